```python
import math
import jax, jax.numpy as jnp
from jax import lax
import numpy as np

D_MODEL = 1024
BATCH = 16
SEQ = 4096
DEPTH = 2

D_MIX = 2 * D_MODEL
D_CONV = D_MIX // 2
D_LRU = D_MIX - D_CONV
CONV_GROUPS = 8
LRU_HEADS = 8
LRU_HEAD_DIM = D_LRU // LRU_HEADS
CONV_KERNEL = 31
SHORT_CONV = 4
LRU_C = 8.0
D_IN = 3 * D_CONV + 2 * D_LRU
EPS = 1e-6

kernel_name = "hymba_style_conformerconv_rglru_adaln"


def rmsnorm(x, g):
    xf = x.astype(jnp.float32)
    y = xf * lax.rsqrt(jnp.mean(xf * xf, axis=-1, keepdims=True) + EPS)
    return (y * g.astype(jnp.float32)).astype(x.dtype)


def layernorm(x, g, b):
    xf = x.astype(jnp.float32)
    mu = jnp.mean(xf, axis=-1, keepdims=True)
    var = jnp.mean(jnp.square(xf - mu), axis=-1, keepdims=True)
    y = (xf - mu) * lax.rsqrt(var + EPS)
    return (y * g.astype(jnp.float32) + b.astype(jnp.float32)).astype(x.dtype)


def causal_depthwise_conv(v, w, b):
    k, ch = w.shape
    y = lax.conv_general_dilated(
        v, w.astype(v.dtype)[:, None, :], window_strides=(1,),
        padding=[(k - 1, 0)], dimension_numbers=("NWC", "WIO", "NWC"),
        feature_group_count=ch)
    return y + b.astype(v.dtype)


def linear_scan(a, bx):
    def step(h, inp):
        a_t, b_t = inp
        h = a_t * h + b_t
        return h, h
    h0 = jnp.zeros((a.shape[0], a.shape[2]), jnp.float32)
    _, hs = lax.scan(step, h0, (jnp.swapaxes(a, 0, 1), jnp.swapaxes(bx, 0, 1)))
    return jnp.swapaxes(hs, 0, 1)


def setup_inputs(seed: int = 0) -> dict:
    key = jax.random.key(seed)
    ks = jax.random.split(key, 24)
    f32 = jnp.float32
    nrm = lambda k, s, sc: (jax.random.normal(k, s, f32) * sc)
    x = jax.random.normal(ks[0], (BATCH, SEQ, D_MODEL), f32)
    c = jax.random.normal(ks[1], (BATCH, D_MODEL), f32)
    norm_g = 1.0 + nrm(ks[2], (DEPTH, D_MODEL), 0.05)
    mod_w = nrm(ks[3], (DEPTH, D_MODEL, 3 * D_MODEL), 0.5 * D_MODEL ** -0.5)
    mod_b = nrm(ks[4], (DEPTH, 3 * D_MODEL), 0.02)
    w_in = nrm(ks[5], (DEPTH, D_MODEL, D_IN), D_MODEL ** -0.5)
    dw_w = nrm(ks[6], (DEPTH, CONV_KERNEL, D_CONV), CONV_KERNEL ** -0.5)
    dw_b = nrm(ks[7], (DEPTH, D_CONV), 0.02)
    cln_g = 1.0 + nrm(ks[8], (DEPTH, D_CONV), 0.05)
    cln_b = nrm(ks[9], (DEPTH, D_CONV), 0.02)
    pw2_w = nrm(ks[10], (DEPTH, D_CONV, D_CONV), D_CONV ** -0.5)
    pw2_b = nrm(ks[11], (DEPTH, D_CONV), 0.02)
    sc_w = nrm(ks[12], (DEPTH, SHORT_CONV, D_LRU), SHORT_CONV ** -0.5)
    sc_b = nrm(ks[13], (DEPTH, D_LRU), 0.02)
    wr = nrm(ks[14], (DEPTH, LRU_HEADS, LRU_HEAD_DIM, LRU_HEAD_DIM), LRU_HEAD_DIM ** -0.5)
    br = nrm(ks[15], (DEPTH, D_LRU), 0.02)
    wi = nrm(ks[16], (DEPTH, LRU_HEADS, LRU_HEAD_DIM, LRU_HEAD_DIM), LRU_HEAD_DIM ** -0.5)
    bi = nrm(ks[17], (DEPTH, D_LRU), 0.02)
    a0 = jax.random.uniform(ks[18], (DEPTH, D_LRU), f32, 0.9, 0.999)
    s = a0 ** (1.0 / LRU_C)
    lam = jnp.log(s) - jnp.log1p(-s)
    w_out = nrm(ks[19], (DEPTH, D_MIX, D_MODEL), D_MIX ** -0.5)
    final_g = 1.0 + nrm(ks[20], (D_MODEL,), 0.05)
    return {"x": x, "c": c, "norm_g": norm_g, "mod_w": mod_w, "mod_b": mod_b,
            "w_in": w_in, "dw_w": dw_w, "dw_b": dw_b, "cln_g": cln_g, "cln_b": cln_b,
            "pw2_w": pw2_w, "pw2_b": pw2_b, "sc_w": sc_w, "sc_b": sc_b,
            "wr": wr, "br": br, "wi": wi, "bi": bi, "lam": lam,
            "w_out": w_out, "final_g": final_g}


def reference(x, c, norm_g, mod_w, mod_b, w_in, dw_w, dw_b, cln_g, cln_b,
              pw2_w, pw2_b, sc_w, sc_b, wr, br, wi, bi, lam, w_out, final_g):
    bsz, seq, _ = x.shape
    c_act = jax.nn.silu(c)
    for l in range(DEPTH):
        mod = c_act @ mod_w[l] + mod_b[l]
        shift, scale, gate = jnp.split(mod, 3, axis=-1)
        h = rmsnorm(x, norm_g[l]) * (1.0 + scale[:, None, :]) + shift[:, None, :]

        u = h @ w_in[l]
        cv, cg, cs, lx, ls = jnp.split(
            u, [D_CONV, 2 * D_CONV, 3 * D_CONV, 3 * D_CONV + D_LRU], axis=-1)

        v = cv * jax.nn.sigmoid(cg)
        v = causal_depthwise_conv(v, dw_w[l], dw_b[l])
        v = jax.nn.silu(layernorm(v, cln_g[l], cln_b[l]))
        v = v @ pw2_w[l] + pw2_b[l]
        y_conv = v * jax.nn.silu(cs)

        xs = causal_depthwise_conv(lx, sc_w[l], sc_b[l])
        xh = xs.reshape(bsz, seq, LRU_HEADS, LRU_HEAD_DIM)
        r = jax.nn.sigmoid(jnp.einsum("bshd,hde->bshe", xh, wr[l]).reshape(bsz, seq, D_LRU) + br[l])
        i = jax.nn.sigmoid(jnp.einsum("bshd,hde->bshe", xh, wi[l]).reshape(bsz, seq, D_LRU) + bi[l])
        log_a = (-LRU_C * r.astype(jnp.float32)
                 * jax.nn.softplus(-lam[l].astype(jnp.float32)))
        a = jnp.exp(log_a)
        mult = jnp.sqrt(-jnp.expm1(2.0 * log_a))
        bx = mult * (i * xs).astype(jnp.float32)
        hs = linear_scan(a, bx).astype(x.dtype)
        y_lru = hs * jax.nn.silu(ls)

        y = jnp.concatenate([y_conv, y_lru], axis=-1) @ w_out[l]
        x = x + (1.0 + gate[:, None, :]) * y
    return rmsnorm(x, final_g)
```

```python
import functools

import jax
import jax.numpy as jnp
from jax import lax
from jax.experimental import pallas as pl
from jax.experimental.pallas import tpu as pltpu

EPS = 1e-6
LRU_C = 8.0
LANES = 128
SUBLANES = 8
TIME_CHUNK = 32
TIME_BLOCK = 8
VMEM_LIMIT_BYTES = 56 * 1024 * 1024

F32 = jnp.float32
BF16 = jnp.bfloat16


def _sigmoid(x):
    return 1.0 / (1.0 + jnp.exp(-x))


def _silu(x):
    return x * _sigmoid(x)


def _mod_kernel(c_ref, w_ref, b_ref, o_ref):
    c = c_ref[...]
    o_ref[0, 0] = jnp.dot(_silu(c), w_ref[0], preferred_element_type=F32,
                          precision=lax.Precision.HIGHEST) + b_ref[0]


def _modulation(c, mod_w, mod_b):
    depth, d, d3 = mod_w.shape
    bsz = c.shape[0]
    n3 = d3 // d
    return pl.pallas_call(
        _mod_kernel,
        grid=(depth, n3),
        in_specs=[
            pl.BlockSpec((bsz, d), lambda l, j: (0, 0)),
            pl.BlockSpec((1, d, d), lambda l, j: (l, 0, j)),
            pl.BlockSpec((1, 1, d), lambda l, j: (l, 0, j)),
        ],
        out_specs=pl.BlockSpec((1, 1, bsz, d), lambda l, j: (l, j, 0, 0)),
        out_shape=jax.ShapeDtypeStruct((depth, n3, bsz, d), F32),
    )(c, mod_w, mod_b.reshape(depth, 1, d3))


def _layer_kernel(x_ref, mod_ref, ng_ref, win_ref, dww_ref, dwb_ref, clng_ref,
                  clnb_ref, pw2w_ref, pw2b_ref, scw_ref, scb_ref, wri_ref,
                  br_ref, bi_ref, lam_ref, wout_ref, fg_ref,
                  o_ref,
                  act_bf, t0, t1, cs_scr, ls_scr, v_scr, lx_scr, ri_scr, ycat,
                  h_carry, *, final_norm):
    tc, bsz, d = x_ref.shape
    m = tc * bsz
    d_conv = pw2w_ref.shape[0]
    d_lru = lam_ref.shape[1]
    k_conv = dww_ref.shape[0]
    k_short = scw_ref.shape[0]
    halo_v = k_conv - 1
    halo_l = k_short - 1
    n_heads, head_dim, _ = wri_ref.shape

    @pl.when(pl.program_id(0) == 0)
    def _():
        v_scr[pl.ds(0, halo_v)] = jnp.zeros((halo_v, bsz, d_conv), F32)
        lx_scr[pl.ds(0, halo_l)] = jnp.zeros((halo_l, bsz, d_lru), F32)
        h_carry[...] = jnp.zeros((bsz, d_lru), F32)

    def rows(t):
        return pl.ds(pl.multiple_of(t * bsz, bsz), bsz)

    def norm_body(t, carry):
        x = x_ref[t]
        ms = jnp.mean(x * x, axis=-1, keepdims=True)
        xn = (x * lax.rsqrt(ms + EPS)) * ng_ref[...]
        h = xn * (1.0 + mod_ref[1]) + mod_ref[0]
        act_bf[rows(t), :] = h.astype(BF16)
        return carry
    lax.fori_loop(0, tc, norm_body, 0)

    def in_proj(col0, width):
        return jnp.dot(act_bf[...], win_ref[:, col0:col0 + width],
                       preferred_element_type=F32).reshape(tc, bsz, width)

    t1[...] = in_proj(0, d_conv)
    t0[...] = in_proj(d_conv, d_conv)

    def glu_body(t, carry):
        v_scr[halo_v + t] = t1[t] * _sigmoid(t0[t])
        return carry
    lax.fori_loop(0, tc, glu_body, 0)

    cs_scr[...] = in_proj(2 * d_conv, d_conv)
    lx_scr[pl.ds(halo_l, tc)] = in_proj(3 * d_conv, d_lru)
    ls_scr[...] = in_proj(3 * d_conv + d_lru, d_lru)

    def conv_body(tb, carry):
        tb0 = pl.multiple_of(tb * TIME_BLOCK, TIME_BLOCK)
        for s in range(d_conv // LANES):
            sl = slice(s * LANES, (s + 1) * LANES)
            acc = jnp.zeros((TIME_BLOCK, bsz, LANES), F32)
            for j in range(k_conv):
                w = dww_ref[j:j + 1, sl]
                acc = acc + w * v_scr[pl.ds(tb0 + j, TIME_BLOCK), :, sl]
            t0[pl.ds(tb0, TIME_BLOCK), :, sl] = acc + dwb_ref[:, sl]
        return carry
    lax.fori_loop(0, tc // TIME_BLOCK, conv_body, 0)

    def ln_body(t, carry):
        v = t0[t]
        mu = jnp.mean(v, axis=-1, keepdims=True)
        dv = v - mu
        var = jnp.mean(dv * dv, axis=-1, keepdims=True)
        y = (dv * lax.rsqrt(var + EPS)) * clng_ref[...] + clnb_ref[...]
        act_bf[rows(t), :] = _silu(y).astype(BF16)
        return carry
    lax.fori_loop(0, tc, ln_body, 0)

    t1[...] = jnp.dot(act_bf[...], pw2w_ref[...],
                      preferred_element_type=F32).reshape(tc, bsz, d_conv)

    def conv_out_body(t, carry):
        y = (t1[t] + pw2b_ref[...]) * _silu(cs_scr[t])
        ycat[rows(t), 0:d_conv] = y.astype(BF16)
        return carry
    lax.fori_loop(0, tc, conv_out_body, 0)

    def short_conv_body(tb, carry):
        tb0 = pl.multiple_of(tb * TIME_BLOCK, TIME_BLOCK)
        for s in range(d_lru // LANES):
            sl = slice(s * LANES, (s + 1) * LANES)
            acc = jnp.zeros((TIME_BLOCK, bsz, LANES), F32)
            for j in range(k_short):
                w = scw_ref[j:j + 1, sl]
                acc = acc + w * lx_scr[pl.ds(tb0 + j, TIME_BLOCK), :, sl]
            xs = acc + scb_ref[:, sl]
            t0[pl.ds(tb0, TIME_BLOCK), :, sl] = xs
            r0 = pl.multiple_of(tb0 * bsz, TIME_BLOCK * bsz)
            act_bf[pl.ds(r0, TIME_BLOCK * bsz), sl] = (
                xs.reshape(TIME_BLOCK * bsz, LANES).astype(BF16))
        return carry
    lax.fori_loop(0, tc // TIME_BLOCK, short_conv_body, 0)

    for hd in range(n_heads):
        c0 = hd * head_dim
        res = jnp.dot(act_bf[:, c0:c0 + head_dim], wri_ref[hd],
                      preferred_element_type=F32).reshape(tc, bsz, 2 * head_dim)
        ri_scr[:, :, c0:c0 + head_dim] = res[:, :, 0:head_dim]
        ri_scr[:, :, d_lru + c0:d_lru + c0 + head_dim] = res[:, :, head_dim:]

    def scan_body(t, h):
        lam = lam_ref[...]
        sp = jnp.maximum(-lam, 0.0) + jnp.log1p(jnp.exp(-jnp.abs(lam)))
        ri = ri_scr[t]
        r = _sigmoid(ri[:, 0:d_lru] + br_ref[...])
        i = _sigmoid(ri[:, d_lru:] + bi_ref[...])
        log_a = (-LRU_C * r) * sp
        a = jnp.exp(log_a)
        mult = jnp.sqrt(1.0 - a * a)
        bx = mult * (i * t0[t])
        h = a * h + bx
        ycat[rows(t), d_conv:] = (h * _silu(ls_scr[t])).astype(BF16)
        return h
    h_carry[...] = lax.fori_loop(0, tc, scan_body, h_carry[...])

    t1[...] = jnp.dot(ycat[...], wout_ref[...],
                      preferred_element_type=F32).reshape(tc, bsz, d)

    def out_body(t, carry):
        xo = x_ref[t] + (1.0 + mod_ref[2]) * t1[t]
        if final_norm:
            ms = jnp.mean(xo * xo, axis=-1, keepdims=True)
            xo = (xo * lax.rsqrt(ms + EPS)) * fg_ref[...]
        o_ref[t] = xo
        return carry
    lax.fori_loop(0, tc, out_body, 0)

    v_scr[pl.ds(0, halo_v)] = v_scr[pl.ds(tc, halo_v)]
    lx_scr[pl.ds(0, halo_l)] = lx_scr[pl.ds(tc, halo_l)]


def _resident(shape):
    zeros = (0,) * len(shape)
    return pl.BlockSpec(shape, lambda j: zeros, pipeline_mode=pl.Buffered(1))


def _layer(xt, mod, ng, win, dww, dwb, clng, clnb, pw2w, pw2b, scw, scb, wri,
           br, bi, lam, wout, fg, *, final_norm):
    seq, bsz, d = xt.shape
    d_conv = pw2w.shape[0]
    d_lru = lam.shape[1]
    tc = TIME_CHUNK
    assert seq % tc == 0 and tc % TIME_BLOCK == 0
    assert tc >= dww.shape[0] - 1 and tc >= scw.shape[0] - 1
    assert bsz % (2 * SUBLANES) == 0 and d_conv % LANES == 0 and d_lru % LANES == 0
    m = tc * bsz
    operands = (mod, ng, win, dww, dwb, clng, clnb, pw2w, pw2b, scw, scb, wri,
                br, bi, lam, wout, fg)
    return pl.pallas_call(
        functools.partial(_layer_kernel, final_norm=final_norm),
        grid=(seq // tc,),
        in_specs=[pl.BlockSpec((tc, bsz, d), lambda j: (j, 0, 0))]
        + [_resident(a.shape) for a in operands],
        out_specs=pl.BlockSpec((tc, bsz, d), lambda j: (j, 0, 0)),
        out_shape=jax.ShapeDtypeStruct(xt.shape, xt.dtype),
        scratch_shapes=[
            pltpu.VMEM((m, max(d, d_conv, d_lru)), BF16),
            pltpu.VMEM((tc, bsz, max(d, d_conv, d_lru)), F32),
            pltpu.VMEM((tc, bsz, max(d, d_conv)), F32),
            pltpu.VMEM((tc, bsz, d_conv), F32),
            pltpu.VMEM((tc, bsz, d_lru), F32),
            pltpu.VMEM((tc + dww.shape[0] - 1, bsz, d_conv), F32),
            pltpu.VMEM((tc + scw.shape[0] - 1, bsz, d_lru), F32),
            pltpu.VMEM((tc, bsz, 2 * d_lru), F32),
            pltpu.VMEM((m, d_conv + d_lru), BF16),
            pltpu.VMEM((bsz, d_lru), F32),
        ],
        compiler_params=pltpu.CompilerParams(
            dimension_semantics=("arbitrary",),
            vmem_limit_bytes=VMEM_LIMIT_BYTES),
    )(xt, *operands)


def kernel(x, c, norm_g, mod_w, mod_b, w_in, dw_w, dw_b, cln_g, cln_b, pw2_w,
           pw2_b, sc_w, sc_b, wr, br, wi, bi, lam, w_out, final_g):
    depth = norm_g.shape[0]
    mod = _modulation(c, mod_w, mod_b)
    xt = jnp.transpose(x, (1, 0, 2))
    row = lambda a: a.reshape(1, -1)
    fg = row(final_g)
    for l in range(depth):
        wri = jnp.concatenate([wr[l], wi[l]], axis=-1).astype(BF16)
        xt = _layer(
            xt, mod[l], row(norm_g[l]), w_in[l].astype(BF16), dw_w[l],
            row(dw_b[l]), row(cln_g[l]), row(cln_b[l]), pw2_w[l].astype(BF16),
            row(pw2_b[l]), sc_w[l], row(sc_b[l]), wri, row(br[l]), row(bi[l]),
            row(lam[l]), w_out[l].astype(BF16), fg,
            final_norm=(l == depth - 1))
    return jnp.transpose(xt, (1, 0, 2))
```

```python
import functools

import jax
import jax.numpy as jnp
from jax import lax
from jax.experimental import pallas as pl
from jax.experimental.pallas import tpu as pltpu

EPS = 1e-6
LRU_C = 8.0
LANES = 128
SUBLANES = 8
MXU_COLS = 256
TIME_CHUNK = 32
TIME_BLOCK = 8
VMEM_LIMIT_BYTES = 58 * 1024 * 1024

F32 = jnp.float32
BF16 = jnp.bfloat16


def _sigmoid(x):
    return 0.5 * jnp.tanh(0.5 * x) + 0.5


def _silu(x):
    h = 0.5 * x
    return h * jnp.tanh(h) + h


def _emit_interleaved(mxu_tasks, valu_tasks):
    ia = ib = 0
    ca = cb = 0.0
    while ia < len(mxu_tasks) or ib < len(valu_tasks):
        if ib >= len(valu_tasks) or (ia < len(mxu_tasks) and ca <= cb):
            cost, fn = mxu_tasks[ia]
            ia += 1
            ca += cost
        else:
            cost, fn = valu_tasks[ib]
            ib += 1
            cb += cost
        fn()


def _mod_kernel(c_ref, w_ref, b_ref, o_ref):
    c = c_ref[...]
    o_ref[0, 0] = jnp.dot(_silu(c), w_ref[0], preferred_element_type=F32,
                          precision=lax.Precision.HIGHEST) + b_ref[0]


def _modulation(c, mod_w, mod_b):
    depth, d, d3 = mod_w.shape
    bsz = c.shape[0]
    n3 = d3 // d
    return pl.pallas_call(
        _mod_kernel,
        grid=(depth, n3),
        in_specs=[
            pl.BlockSpec((bsz, d), lambda l, j: (0, 0)),
            pl.BlockSpec((1, d, d), lambda l, j: (l, 0, j)),
            pl.BlockSpec((1, 1, d), lambda l, j: (l, 0, j)),
        ],
        out_specs=pl.BlockSpec((1, 1, bsz, d), lambda l, j: (l, j, 0, 0)),
        out_shape=jax.ShapeDtypeStruct((depth, n3, bsz, d), F32),
        name="modulation",
    )(c, mod_w, mod_b.reshape(depth, 1, d3))


def _layer_kernel(x_ref, mod_ref, ng_ref, win_ref, dww_ref, dwb_ref, clng_ref,
                  clnb_ref, pw2w_ref, pw2b_ref, scw_ref, scb_ref, wri_ref,
                  br_ref, bi_ref, lam_ref, wout_ref, fg_ref,
                  o_ref,
                  h_bf, ln_bf, xs_bf, cvg, conv_scr, xs_scr, t1, cs_scr, ls_scr,
                  v_scr, lx_scr, ri_scr, ycat, sp_scr, h_carry, *, final_norm):
    tc, bsz, d = x_ref.shape
    d_conv = pw2w_ref.shape[0]
    d_lru = lam_ref.shape[1]
    k_conv = dww_ref.shape[0]
    k_short = scw_ref.shape[0]
    halo_v = k_conv - 1
    halo_l = k_short - 1
    n_heads, head_dim, _ = wri_ref.shape
    cc = MXU_COLS
    n_chunks = d_conv // cc
    strips_per_chunk = cc // LANES
    n_tb = tc // TIME_BLOCK

    @pl.when(pl.program_id(0) == 0)
    def _():
        v_scr[0:halo_v] = jnp.zeros((halo_v, bsz, d_conv), F32)
        lx_scr[0:halo_l] = jnp.zeros((halo_l, bsz, d_lru), F32)
        h_carry[...] = jnp.zeros((bsz, d_lru), F32)

    def rows(t, n=1):
        return slice(t * bsz, (t + n) * bsz)

    def chunk(c):
        return slice(c * cc, (c + 1) * cc)

    def strip(s):
        return slice(s * LANES, (s + 1) * LANES)

    def norm(t):
        x = x_ref[t]
        ms = jnp.mean(x * x, axis=-1, keepdims=True)
        xn = (x * lax.rsqrt(ms + EPS)) * ng_ref[...]
        h = xn * (1.0 + mod_ref[1]) + mod_ref[0]
        h_bf[rows(t), :] = h.astype(BF16)

    def in_dot(col0):
        return jnp.dot(h_bf[...], win_ref[:, col0:col0 + cc],
                       preferred_element_type=F32).reshape(tc, bsz, cc)

    def cv_dot(c):
        cvg[c % 2, 0] = in_dot(c * cc)

    def cg_dot(c):
        cvg[c % 2, 1] = in_dot(d_conv + c * cc)

    def cs_dot(c):
        cs_scr[:, :, chunk(c)] = in_dot(2 * d_conv + c * cc)

    def lx_dot(c):
        lx_scr[halo_l:halo_l + tc, :, chunk(c)] = in_dot(3 * d_conv + c * cc)

    def ls_dot(c):
        ls_scr[:, :, chunk(c)] = in_dot(3 * d_conv + d_lru + c * cc)

    def glu(c, tb):
        t0 = tb * TIME_BLOCK
        tt = slice(t0, t0 + TIME_BLOCK)
        v_scr[halo_v + t0:halo_v + t0 + TIME_BLOCK, :, chunk(c)] = (
            cvg[c % 2, 0, tt] * _sigmoid(cvg[c % 2, 1, tt]))

    def conv(tb, s):
        t0 = tb * TIME_BLOCK
        sl = strip(s)
        acc = dww_ref[0:1, sl] * v_scr[t0:t0 + TIME_BLOCK, :, sl]
        for j in range(1, k_conv):
            acc = acc + dww_ref[j:j + 1, sl] * v_scr[t0 + j:t0 + j + TIME_BLOCK, :, sl]
        conv_scr[t0:t0 + TIME_BLOCK, :, sl] = acc + dwb_ref[:, sl]

    def short_conv(tb, s):
        t0 = tb * TIME_BLOCK
        sl = strip(s)
        acc = scw_ref[0:1, sl] * lx_scr[t0:t0 + TIME_BLOCK, :, sl]
        for j in range(1, k_short):
            acc = acc + scw_ref[j:j + 1, sl] * lx_scr[t0 + j:t0 + j + TIME_BLOCK, :, sl]
        xs = acc + scb_ref[:, sl]
        xs_scr[t0:t0 + TIME_BLOCK, :, sl] = xs
        xs_bf[rows(t0, TIME_BLOCK), sl] = (
            xs.reshape(TIME_BLOCK * bsz, LANES).astype(BF16))

    def gate_dot(hd):
        c0 = hd * head_dim
        res = jnp.dot(xs_bf[:, c0:c0 + head_dim], wri_ref[hd],
                      preferred_element_type=F32).reshape(tc, bsz, 2 * head_dim)
        ri_scr[:, :, c0:c0 + head_dim] = res[:, :, 0:head_dim]
        ri_scr[:, :, d_lru + c0:d_lru + c0 + head_dim] = res[:, :, head_dim:]

    def layer_norm(t):
        v = conv_scr[t]
        mu = jnp.mean(v, axis=-1, keepdims=True)
        dv = v - mu
        var = jnp.mean(dv * dv, axis=-1, keepdims=True)
        y = (dv * lax.rsqrt(var + EPS)) * clng_ref[...] + clnb_ref[...]
        ln_bf[rows(t), :] = _silu(y).astype(BF16)

    def pw_dot(c):
        t1[:, :, chunk(c)] = jnp.dot(
            ln_bf[...], pw2w_ref[:, chunk(c)],
            preferred_element_type=F32).reshape(tc, bsz, cc)

    def conv_out(t):
        y = (t1[t] + pw2b_ref[...]) * _silu(cs_scr[t])
        ycat[rows(t), 0:d_conv] = y.astype(BF16)

    state = [None]

    def scan(t):
        ri = ri_scr[t]
        r = _sigmoid(ri[:, 0:d_lru] + br_ref[...])
        i = _sigmoid(ri[:, d_lru:] + bi_ref[...])
        log_a = r * sp_scr[...]
        a = jnp.exp(log_a)
        mult = jnp.sqrt(1.0 - a * a)
        h = a * state[0] + mult * (i * xs_scr[t])
        state[0] = h
        ycat[rows(t), d_conv:] = (h * _silu(ls_scr[t])).astype(BF16)

    def out_dot_conv(c):
        t1[:, :, chunk(c)] = jnp.dot(
            ycat[:, 0:d_conv], wout_ref[0:d_conv, chunk(c)],
            preferred_element_type=F32).reshape(tc, bsz, cc)

    def out_dot_lru(c):
        t1[:, :, chunk(c)] += jnp.dot(
            ycat[:, d_conv:], wout_ref[d_conv:, chunk(c)],
            preferred_element_type=F32).reshape(tc, bsz, cc)

    def out(t):
        xo = x_ref[t] + (1.0 + mod_ref[2]) * t1[t]
        if final_norm:
            ms = jnp.mean(xo * xo, axis=-1, keepdims=True)
            xo = (xo * lax.rsqrt(ms + EPS)) * fg_ref[...]
        o_ref[t] = xo

    dot_cost = 512.0
    gate_cost = 256.0
    conv_cost = 274.0
    sc_cost = 35.0
    glu_cost = 50.0

    for t in range(tc):
        norm(t)

    lam = lam_ref[...]
    sp_scr[...] = -LRU_C * (jnp.maximum(-lam, 0.0)
                            + jnp.log1p(jnp.exp(-jnp.abs(lam))))

    cv_dot(0)
    cg_dot(0)

    lru_in = ([(dot_cost, functools.partial(lx_dot, c)) for c in range(n_chunks)]
              + [(dot_cost, functools.partial(cs_dot, c)) for c in range(n_chunks)])
    tail_in = [(dot_cost, functools.partial(ls_dot, c)) for c in range(n_chunks)]
    gates = [(gate_cost, functools.partial(gate_dot, hd)) for hd in range(n_heads)]
    sc_units = [[(sc_cost, functools.partial(short_conv, tb, s))
                 for tb in range(n_tb)] for s in range(d_lru // LANES)]
    per_round = len(lru_in) // (n_chunks - 1) if n_chunks > 1 else len(lru_in)

    for r in range(n_chunks):
        mxu = []
        if r + 1 < n_chunks:
            mxu += [(dot_cost, functools.partial(cv_dot, r + 1)),
                    (dot_cost, functools.partial(cg_dot, r + 1))]
            mxu += lru_in[r * per_round:(r + 1) * per_round]
            if r + 2 == n_chunks:
                mxu += lru_in[(r + 1) * per_round:]
        valu = [(glu_cost, functools.partial(glu, r, tb)) for tb in range(n_tb)]
        for s in range(r * strips_per_chunk, (r + 1) * strips_per_chunk):
            valu += [(conv_cost, functools.partial(conv, tb, s)) for tb in range(n_tb)]
        n_lru_strips = d_lru // LANES
        if r == 1:
            for s in range(0, n_lru_strips // 2):
                valu += sc_units[s]
        if r == 2:
            for s in range(n_lru_strips // 2, n_lru_strips):
                valu += sc_units[s]
            mxu += gates[:n_heads // 2]
        if r == 3:
            mxu += tail_in + gates[n_heads // 2:]
        _emit_interleaved(mxu, valu)

    for t in range(tc):
        layer_norm(t)

    state[0] = h_carry[...]
    half = tc // 2
    _emit_interleaved(
        [(dot_cost, functools.partial(pw_dot, c)) for c in range(n_chunks)],
        [(90.0, functools.partial(scan, t)) for t in range(half)])
    for t in range(tc):
        conv_out(t)
    _emit_interleaved(
        [(dot_cost, functools.partial(out_dot_conv, c)) for c in range(d // cc)],
        [(90.0, functools.partial(scan, t)) for t in range(half, tc)])
    h_carry[...] = state[0]
    for c in range(d // cc):
        out_dot_lru(c)
    for t in range(tc):
        out(t)

    v_scr[0:halo_v] = v_scr[tc:tc + halo_v]
    lx_scr[0:halo_l] = lx_scr[tc:tc + halo_l]


def _resident(shape):
    zeros = (0,) * len(shape)
    return pl.BlockSpec(shape, lambda j: zeros, pipeline_mode=pl.Buffered(1))


def _layer(xt, mod, ng, win, dww, dwb, clng, clnb, pw2w, pw2b, scw, scb, wri,
           br, bi, lam, wout, fg, *, final_norm):
    seq, bsz, d = xt.shape
    d_conv = pw2w.shape[0]
    d_lru = lam.shape[1]
    tc = TIME_CHUNK
    assert seq % tc == 0 and tc % (2 * TIME_BLOCK) == 0
    assert tc >= dww.shape[0] - 1 and tc >= scw.shape[0] - 1
    assert bsz % (2 * SUBLANES) == 0
    assert d % MXU_COLS == 0 and d_conv == d_lru and d_conv // MXU_COLS == 4
    m = tc * bsz
    operands = (mod, ng, win, dww, dwb, clng, clnb, pw2w, pw2b, scw, scb, wri,
                br, bi, lam, wout, fg)
    return pl.pallas_call(
        functools.partial(_layer_kernel, final_norm=final_norm),
        grid=(seq // tc,),
        in_specs=[pl.BlockSpec((tc, bsz, d), lambda j: (j, 0, 0))]
        + [_resident(a.shape) for a in operands],
        out_specs=pl.BlockSpec((tc, bsz, d), lambda j: (j, 0, 0)),
        out_shape=jax.ShapeDtypeStruct(xt.shape, xt.dtype),
        scratch_shapes=[
            pltpu.VMEM((m, d), BF16),
            pltpu.VMEM((m, d_conv), BF16),
            pltpu.VMEM((m, d_lru), BF16),
            pltpu.VMEM((2, 2, tc, bsz, MXU_COLS), F32),
            pltpu.VMEM((tc, bsz, d_conv), F32),
            pltpu.VMEM((tc, bsz, d_lru), F32),
            pltpu.VMEM((tc, bsz, max(d, d_conv)), F32),
            pltpu.VMEM((tc, bsz, d_conv), F32),
            pltpu.VMEM((tc, bsz, d_lru), F32),
            pltpu.VMEM((tc + dww.shape[0] - 1, bsz, d_conv), F32),
            pltpu.VMEM((tc + scw.shape[0] - 1, bsz, d_lru), F32),
            pltpu.VMEM((tc, bsz, 2 * d_lru), F32),
            pltpu.VMEM((m, d_conv + d_lru), BF16),
            pltpu.VMEM((1, d_lru), F32),
            pltpu.VMEM((bsz, d_lru), F32),
        ],
        compiler_params=pltpu.CompilerParams(
            dimension_semantics=("arbitrary",),
            vmem_limit_bytes=VMEM_LIMIT_BYTES),
        name="final_layer" if final_norm else "layer",
    )(xt, *operands)


def kernel(x, c, norm_g, mod_w, mod_b, w_in, dw_w, dw_b, cln_g, cln_b, pw2_w,
           pw2_b, sc_w, sc_b, wr, br, wi, bi, lam, w_out, final_g):
    depth = norm_g.shape[0]
    mod = _modulation(c, mod_w, mod_b)
    xt = jnp.transpose(x, (1, 0, 2))
    row = lambda a: a.reshape(1, -1)
    fg = row(final_g)
    for l in range(depth):
        wri = jnp.concatenate([wr[l], wi[l]], axis=-1).astype(BF16)
        xt = _layer(
            xt, mod[l], row(norm_g[l]), w_in[l].astype(BF16), dw_w[l],
            row(dw_b[l]), row(cln_g[l]), row(cln_b[l]), pw2_w[l].astype(BF16),
            row(pw2_b[l]), sc_w[l], row(sc_b[l]), wri, row(br[l]), row(bi[l]),
            row(lam[l]), w_out[l].astype(BF16), fg,
            final_norm=(l == depth - 1))
    return jnp.transpose(xt, (1, 0, 2))
```

```python
import functools

import jax
import jax.numpy as jnp
from jax import lax
from jax.experimental import pallas as pl
from jax.experimental.pallas import tpu as pltpu

EPS = 1e-6
LRU_C = 8.0
LANES = 128
SUBLANES = 8
MXU_COLS = 256
TIME_CHUNK = 32
TIME_BLOCK = 8
VMEM_LIMIT_BYTES = 58 * 1024 * 1024

F32 = jnp.float32
BF16 = jnp.bfloat16

_GROUPS = ("cv", "cg", "cs", "lx", "ls")
_N = 4
_CHUNK_ORDER = (
    ("cv", 0), ("cg", 0), ("lx", 0), ("lx", 1),
    ("cv", 1), ("cg", 1), ("lx", 2), ("lx", 3),
    ("cv", 2), ("cg", 2), ("cs", 0), ("cs", 1),
    ("cv", 3), ("cg", 3), ("cs", 2), ("cs", 3),
    ("ls", 0), ("ls", 1), ("ls", 2), ("ls", 3),
)
_SLOT = {key: p for p, key in enumerate(_CHUNK_ORDER)}
_DOTS_PER_ROUND = 4
_FIRST_SLOTS = (0, 1)
_LAST_SLOTS = (18, 19)


def _sigmoid(x):
    return 0.5 * jnp.tanh(0.5 * x) + 0.5


def _silu(x):
    h = 0.5 * x
    return h * jnp.tanh(h) + h


def _mod_kernel(c_ref, w_ref, b_ref, o_ref):
    c = c_ref[...]
    o_ref[0, 0] = jnp.dot(_silu(c), w_ref[0], preferred_element_type=F32,
                          precision=lax.Precision.HIGHEST) + b_ref[0]


def _modulation(c, mod_w, mod_b):
    depth, d, d3 = mod_w.shape
    bsz = c.shape[0]
    n3 = d3 // d
    return pl.pallas_call(
        _mod_kernel,
        grid=(depth, n3),
        in_specs=[
            pl.BlockSpec((bsz, d), lambda l, j: (0, 0)),
            pl.BlockSpec((1, d, d), lambda l, j: (l, 0, j)),
            pl.BlockSpec((1, 1, d), lambda l, j: (l, 0, j)),
        ],
        out_specs=pl.BlockSpec((1, 1, bsz, d), lambda l, j: (l, j, 0, 0)),
        out_shape=jax.ShapeDtypeStruct((depth, n3, bsz, d), F32),
        name="modulation",
    )(c, mod_w, mod_b.reshape(depth, 1, d3))


def _layer_kernel(x_ref, mod_ref, ng_ref, win_ref, dww_ref, dwb_ref, clng_ref,
                  clnb_ref, pw2w_ref, pw2b_ref, scw_ref, scb_ref, wri_ref,
                  br_ref, bi_ref, lam_ref, wout_ref, fg_ref,
                  o_ref,
                  act_bf, xs_bf, u_scr, v_scr, conv_scr, lx_halo, ri_scr, t1c, t2c,
                  y_conv, y_lru, sp_scr, h_carry, *, final_norm):
    tc, bsz, d = x_ref.shape
    k_conv = dww_ref.shape[1]
    k_short = scw_ref.shape[0]
    halo_v = k_conv - 1
    halo_l = k_short - 1
    n_heads = wri_ref.shape[0]
    d_lru = lam_ref.shape[1]
    cc = MXU_COLS
    d_conv = _N * cc
    spc = cc // LANES
    n_tb = tc // TIME_BLOCK
    n_out = d // cc

    @pl.when(pl.program_id(0) == 0)
    def _():
        v_scr[:, 0:halo_v] = jnp.zeros((_N, halo_v, bsz, cc), F32)
        lx_halo[...] = jnp.zeros(lx_halo.shape, F32)
        h_carry[...] = jnp.zeros((bsz, d_lru), F32)

    def rows(t, n=1):
        if isinstance(t, int):
            return slice(t * bsz, (t + n) * bsz)
        return pl.ds(pl.multiple_of(t * bsz, bsz), n * bsz)

    def chunk(c):
        return slice(c * cc, (c + 1) * cc)

    def strip(s):
        return slice(s * LANES, (s + 1) * LANES)

    def matmul(lhs, packed_w):
        res = jnp.dot(lhs, pltpu.bitcast(packed_w, BF16),
                      preferred_element_type=F32)
        return res.reshape(tc, bsz, res.shape[-1])

    def in_dot(p):
        u_scr[p] = matmul(act_bf[...], win_ref[p])

    def causal_taps(get_w, n_taps, get_slab):
        w = [jnp.broadcast_to(get_w(j), (bsz, LANES)) for j in range(n_taps)]
        acc = [None] * TIME_BLOCK
        for u in range(TIME_BLOCK + n_taps - 1):
            slab = get_slab(u)
            for i in range(TIME_BLOCK):
                j = u - i
                if 0 <= j < n_taps:
                    term = w[j] * slab
                    acc[i] = term if acc[i] is None else acc[i] + term
        return acc

    def norm_body(t, carry):
        x = x_ref[t]
        ms = jnp.mean(x * x, axis=-1, keepdims=True)
        xn = (x * lax.rsqrt(ms + EPS)) * ng_ref[...]
        h = xn * (1.0 + mod_ref[1]) + mod_ref[0]
        act_bf[rows(t), :] = h.astype(BF16)
        return carry
    lax.fori_loop(0, tc, norm_body, 0, unroll=8)

    lam = lam_ref[...]
    sp_scr[...] = -LRU_C * (jnp.maximum(-lam, 0.0)
                            + jnp.log1p(jnp.exp(-jnp.abs(lam))))

    for p in _FIRST_SLOTS:
        in_dot(p)

    def round_body(r, carry):
        cv = _DOTS_PER_ROUND * r
        for tb in range(n_tb):
            tt = slice(tb * TIME_BLOCK, (tb + 1) * TIME_BLOCK)
            v_scr[r, halo_v + tb * TIME_BLOCK:halo_v + (tb + 1) * TIME_BLOCK] = (
                u_scr[cv, tt] * _sigmoid(u_scr[cv + 1, tt]))
        for tb in range(n_tb):
            t0 = tb * TIME_BLOCK
            for half in range(spc):
                sl = strip(half)
                acc = causal_taps(lambda j: dww_ref[r, j:j + 1, sl], k_conv,
                                  lambda u: v_scr[r, t0 + u, :, sl])
                for k in range(TIME_BLOCK):
                    conv_scr[r, t0 + k, :, sl] = acc[k] + dwb_ref[r, :, sl]
        for q in range(_DOTS_PER_ROUND):
            in_dot(cv + 2 + q)
        return carry
    lax.fori_loop(0, _N, round_body, 0)

    v_scr[:, 0:halo_v] = v_scr[:, tc:tc + halo_v]

    for c in range(_N):
        lx = _SLOT["lx", c]
        for half in range(spc):
            s = c * spc + half
            for tb in range(n_tb):
                t0 = tb * TIME_BLOCK

                def get_slab(u, t0=t0, c=c, lx=lx, half=half):
                    t = t0 + u - halo_l
                    if t < 0:
                        return lx_halo[c, halo_l + t, :, strip(half)]
                    return u_scr[lx, t, :, strip(half)]
                acc = causal_taps(lambda j, s=s: scw_ref[j:j + 1, strip(s)],
                                  k_short, get_slab)
                for k in range(TIME_BLOCK):
                    xs = acc[k] + scb_ref[:, strip(s)]
                    u_scr[_SLOT["cg", c], t0 + k, :, strip(half)] = xs
                    xs_bf[s, rows(t0 + k), :] = xs.astype(BF16)
        lx_halo[c] = u_scr[lx, tc - halo_l:tc]
    for p in _LAST_SLOTS:
        in_dot(p)

    ln_trips = 2
    ln_slabs = tc // ln_trips
    ln_heads = n_heads // ln_trips

    def ln_body(i, carry):
        def one_slab(k, carry):
            t = i * ln_slabs + k
            v = [conv_scr[c, t] for c in range(_N)]
            tot = v[0]
            for c in range(1, _N):
                tot = tot + v[c]
            mu = jnp.sum(tot, axis=-1, keepdims=True) * (1.0 / d_conv)
            dv = [vc - mu for vc in v]
            sq = dv[0] * dv[0]
            for c in range(1, _N):
                sq = sq + dv[c] * dv[c]
            var = jnp.sum(sq, axis=-1, keepdims=True) * (1.0 / d_conv)
            inv = lax.rsqrt(var + EPS)
            for c in range(_N):
                y = (dv[c] * inv) * clng_ref[:, chunk(c)] + clnb_ref[:, chunk(c)]
                act_bf[rows(t), chunk(c)] = _silu(y).astype(BF16)
            return carry
        for k in range(ln_slabs):
            one_slab(k, 0)
        for q in range(ln_heads):
            hd = i * ln_heads + q
            ri_scr[hd] = matmul(xs_bf[hd], wri_ref[hd])
        return carry
    lax.fori_loop(0, ln_trips, ln_body, 0)

    head_dim = d_lru // n_heads
    assert head_dim == LANES

    def scan_block(i, hs, dots):
        hs = list(hs)
        for k in range(TIME_BLOCK):
            t = i * TIME_BLOCK + k
            for hd in range(n_heads):
                hl = strip(hd)
                c, half = divmod(hd, spc)
                ri = ri_scr[hd, t]
                r = _sigmoid(ri[:, 0:head_dim] + br_ref[:, hl])
                g = _sigmoid(ri[:, head_dim:] + bi_ref[:, hl])
                a = jnp.exp(r * sp_scr[:, hl])
                mult = jnp.sqrt(1.0 - a * a)
                xs = u_scr[_SLOT["cg", c], t, :, strip(half)]
                h = a * hs[hd] + mult * (g * xs)
                hs[hd] = h
                ls = u_scr[_SLOT["ls", c], t, :, strip(half)]
                y_lru[rows(t), hl] = (h * _silu(ls)).astype(BF16)
        dots(i)
        return tuple(hs)

    def pointwise_dots(i):
        for q in range(2):
            t1c[2 * i + q] = matmul(act_bf[...], pw2w_ref[2 * i + q])

    def out_conv_dots(i):
        for q in range(2):
            c = 2 * (i - n_tb // 2) + q
            t2c[c] = matmul(y_conv[...], wout_ref[0, c])

    hs = tuple(h_carry[:, strip(hd)] for hd in range(n_heads))
    hs = lax.fori_loop(0, n_tb // 2,
                       functools.partial(scan_block, dots=pointwise_dots), hs)

    def conv_out_body(t, carry):
        for c in range(_N):
            y = ((t1c[c, t] + pw2b_ref[:, chunk(c)])
                 * _silu(u_scr[_SLOT["cs", c], t]))
            y_conv[rows(t), chunk(c)] = y.astype(BF16)
        return carry
    lax.fori_loop(0, tc, conv_out_body, 0, unroll=8)

    hs = lax.fori_loop(n_tb // 2, n_tb,
                       functools.partial(scan_block, dots=out_conv_dots), hs)
    for hd in range(n_heads):
        h_carry[:, strip(hd)] = hs[hd]

    for c in range(n_out):
        t2c[c] += matmul(y_lru[...], wout_ref[1, c])

    def out_body(t, carry):
        xo = [x_ref[t, :, chunk(c)] + (1.0 + mod_ref[2, :, chunk(c)]) * t2c[c, t]
              for c in range(n_out)]
        if final_norm:
            sq = xo[0] * xo[0]
            for c in range(1, n_out):
                sq = sq + xo[c] * xo[c]
            ms = jnp.sum(sq, axis=-1, keepdims=True) * (1.0 / d)
            inv = lax.rsqrt(ms + EPS)
            xo = [(xc * inv) * fg_ref[:, chunk(c)] for c, xc in enumerate(xo)]
        for c, xc in enumerate(xo):
            o_ref[t, :, chunk(c)] = xc
        return carry
    lax.fori_loop(0, tc, out_body, 0, unroll=8)


def _resident(shape):
    zeros = (0,) * len(shape)
    return pl.BlockSpec(shape, lambda j: zeros, pipeline_mode=pl.Buffered(1))


def _layer(xt, mod, ng, win, dww, dwb, clng, clnb, pw2w, pw2b, scw, scb, wri,
           br, bi, lam, wout, fg, *, final_norm):
    seq, bsz, d = xt.shape
    d_lru = lam.shape[1]
    n_heads = wri.shape[0]
    k_conv = dww.shape[1]
    k_short = scw.shape[0]
    tc = TIME_CHUNK
    cc = MXU_COLS
    assert seq % tc == 0 and tc % (2 * TIME_BLOCK) == 0
    assert tc >= k_conv - 1 and tc >= k_short - 1
    assert bsz % (2 * SUBLANES) == 0
    assert d == d_lru == _N * cc and win.shape[0] == len(_CHUNK_ORDER)
    assert dww.shape == (_N, k_conv, cc)
    assert n_heads * LANES == d_lru and n_heads % 2 == 0
    m = tc * bsz
    operands = (mod, ng, win, dww, dwb, clng, clnb, pw2w, pw2b, scw, scb, wri,
                br, bi, lam, wout, fg)
    return pl.pallas_call(
        functools.partial(_layer_kernel, final_norm=final_norm),
        grid=(seq // tc,),
        in_specs=[pl.BlockSpec((tc, bsz, d), lambda j: (j, 0, 0))]
        + [_resident(a.shape) for a in operands],
        out_specs=pl.BlockSpec((tc, bsz, d), lambda j: (j, 0, 0)),
        out_shape=jax.ShapeDtypeStruct(xt.shape, xt.dtype),
        scratch_shapes=[
            pltpu.VMEM((m, d), BF16),
            pltpu.VMEM((n_heads, m, LANES), BF16),
            pltpu.VMEM((len(_CHUNK_ORDER), tc, bsz, cc), F32),
            pltpu.VMEM((_N, tc + k_conv - 1, bsz, cc), F32),
            pltpu.VMEM((_N, tc, bsz, cc), F32),
            pltpu.VMEM((_N, k_short - 1, bsz, cc), F32),
            pltpu.VMEM((n_heads, tc, bsz, 2 * LANES), F32),
            pltpu.VMEM((_N, tc, bsz, cc), F32),
            pltpu.VMEM((d // cc, tc, bsz, cc), F32),
            pltpu.VMEM((m, d_lru), BF16),
            pltpu.VMEM((m, d_lru), BF16),
            pltpu.VMEM((1, d_lru), F32),
            pltpu.VMEM((bsz, d_lru), F32),
        ],
        compiler_params=pltpu.CompilerParams(
            dimension_semantics=("arbitrary",),
            vmem_limit_bytes=VMEM_LIMIT_BYTES),
        name="final_layer" if final_norm else "layer",
    )(xt, *operands)


def _pack_bf16_rows(w):
    *lead, k, n = w.shape
    pairs = w.astype(BF16).reshape(*lead, k // 2, 2, n)
    return lax.bitcast_convert_type(jnp.swapaxes(pairs, -1, -2), jnp.uint32)


def _column_chunks(w):
    k, n = w.shape
    return jnp.swapaxes(w.reshape(k, n // MXU_COLS, MXU_COLS), 0, 1)


def kernel(x, c, norm_g, mod_w, mod_b, w_in, dw_w, dw_b, cln_g, cln_b, pw2_w,
           pw2_b, sc_w, sc_b, wr, br, wi, bi, lam, w_out, final_g):
    depth = norm_g.shape[0]
    d_conv = dw_w.shape[2]
    mod = _modulation(c, mod_w, mod_b)
    xt = jnp.transpose(x, (1, 0, 2))
    row = lambda a: a.reshape(1, -1)
    fg = row(final_g)
    order = jnp.array([_GROUPS.index(g) * _N + i for g, i in _CHUNK_ORDER])
    for l in range(depth):
        win = _pack_bf16_rows(_column_chunks(w_in[l])[order])
        wout = _pack_bf16_rows(jnp.stack([_column_chunks(w_out[l][:d_conv]),
                                          _column_chunks(w_out[l][d_conv:])]))
        wri = _pack_bf16_rows(jnp.concatenate([wr[l], wi[l]], axis=-1))
        xt = _layer(
            xt, mod[l], row(norm_g[l]), win, _column_chunks(dw_w[l]),
            _column_chunks(row(dw_b[l])), row(cln_g[l]), row(cln_b[l]),
            _pack_bf16_rows(_column_chunks(pw2_w[l])), row(pw2_b[l]), sc_w[l],
            row(sc_b[l]), wri, row(br[l]), row(bi[l]), row(lam[l]), wout, fg,
            final_norm=(l == depth - 1))
    return jnp.transpose(xt, (1, 0, 2))
```

```python
import functools

import jax
import jax.numpy as jnp
from jax import lax
from jax.experimental import pallas as pl
from jax.experimental.pallas import tpu as pltpu

EPS = 1e-6
LRU_C = 8.0
LANES = 128
SUBLANES = 8
MXU_COLS = 256
TIME_CHUNK = 32
TIME_BLOCK = 8
VMEM_LIMIT_BYTES = 58 * 1024 * 1024

F32 = jnp.float32
BF16 = jnp.bfloat16


def _half_silu(h):
    return h * jnp.tanh(h) + h


def _emit_interleaved(mxu_tasks, valu_tasks):
    ia = ib = 0
    ca = cb = 0.0
    while ia < len(mxu_tasks) or ib < len(valu_tasks):
        if ib >= len(valu_tasks) or (ia < len(mxu_tasks) and ca <= cb):
            cost, fn = mxu_tasks[ia]
            ia += 1
            ca += cost
        else:
            cost, fn = valu_tasks[ib]
            ib += 1
            cb += cost
        fn()


def _mod_kernel(c_ref, w_ref, b_ref, o_ref):
    c = c_ref[...]
    o_ref[0, 0] = jnp.dot(_half_silu(0.5 * c), w_ref[0], preferred_element_type=F32,
                          precision=lax.Precision.HIGHEST) + b_ref[0]


def _modulation(c, mod_w, mod_b):
    depth, d, d3 = mod_w.shape
    bsz = c.shape[0]
    n3 = d3 // d
    return pl.pallas_call(
        _mod_kernel,
        grid=(depth, n3),
        in_specs=[
            pl.BlockSpec((bsz, d), lambda l, j: (0, 0)),
            pl.BlockSpec((1, d, d), lambda l, j: (l, 0, j)),
            pl.BlockSpec((1, 1, d), lambda l, j: (l, 0, j)),
        ],
        out_specs=pl.BlockSpec((1, 1, bsz, d), lambda l, j: (l, j, 0, 0)),
        out_shape=jax.ShapeDtypeStruct((depth, n3, bsz, d), F32),
        name="modulation",
    )(c, mod_w, mod_b.reshape(depth, 1, d3))


def _layer_kernel(x_in, mod_ref, ng_ref, win_ref, dww_ref, dwb_ref, clng_ref,
                  clnb_ref, pw2w_ref, pw2b_ref, scw_ref, scb_ref, wri_ref,
                  br_ref, bi_ref, lam_ref, wout_ref, fg_ref,
                  o_out,
                  h_bf, ln_bf, xs_bf, cvg, conv_scr, xs_scr, t1, cs_scr, ls_scr,
                  v_scr, lx_scr, ri_scr, ycat, sp_scr, h_carry, *io_scratch,
                  final_norm, batch_major_in, batch_major_out):
    tc, bsz, d = t1.shape
    d_conv = dww_ref.shape[1]
    d_lru = lam_ref.shape[1]
    k_conv = dww_ref.shape[0]
    k_short = scw_ref.shape[0]
    halo_v = k_conv - 1
    halo_l = k_short - 1
    n_heads = wri_ref.shape[0]
    head_dim = d_lru // n_heads
    cc = MXU_COLS
    n_chunks = d_conv // cc
    strips_per_chunk = cc // LANES
    n_tb = tc // TIME_BLOCK
    step = pl.program_id(0)
    n_steps = pl.num_programs(0)
    slot = lax.rem(step, 2)

    io_scratch = list(io_scratch)
    if batch_major_in:
        x_buf, in_sem = io_scratch[:2]
        io_scratch = io_scratch[2:]
    if batch_major_out:
        o_buf, out_sem = io_scratch[:2]

    @pl.when(step == 0)
    def _():
        v_scr[0:halo_v] = jnp.zeros((halo_v, bsz, d_conv), F32)
        lx_scr[0:halo_l] = jnp.zeros((halo_l, bsz, d_lru), F32)
        h_carry[...] = jnp.zeros((bsz, d_lru), F32)

    def in_copies(s, buf_slot):
        return [pltpu.make_async_copy(x_in.at[b, pl.ds(s * tc, tc), :],
                                      x_buf.at[buf_slot, :, b, :],
                                      in_sem.at[buf_slot, b]) for b in range(bsz)]

    def out_copies(s, buf_slot):
        return [pltpu.make_async_copy(o_buf.at[buf_slot, :, b, :],
                                      o_out.at[b, pl.ds(s * tc, tc), :],
                                      out_sem.at[buf_slot, b]) for b in range(bsz)]

    if batch_major_in:
        @pl.when(step == 0)
        def _():
            for cp in in_copies(0, 0):
                cp.start()

        @pl.when(step + 1 < n_steps)
        def _():
            for cp in in_copies(step + 1, 1 - slot):
                cp.start()

        for cp in in_copies(step, slot):
            cp.wait()

    def x_at(t):
        return x_buf[slot, t] if batch_major_in else x_in[t]

    def rows(t, n=1):
        return slice(t * bsz, (t + n) * bsz)

    def chunk(c):
        return slice(c * cc, (c + 1) * cc)

    def strip(s):
        return slice(s * LANES, (s + 1) * LANES)

    def matmul(lhs, packed_w):
        res = jnp.dot(lhs, pltpu.bitcast(packed_w, BF16),
                      preferred_element_type=F32)
        return res.reshape(tc, bsz, res.shape[-1])

    def norm(t):
        x = x_at(t)
        ms = jnp.mean(x * x, axis=-1, keepdims=True)
        xn = (x * lax.rsqrt(ms + EPS)) * ng_ref[...]
        h = xn * (1.0 + mod_ref[1]) + mod_ref[0]
        h_bf[rows(t), :] = h.astype(BF16)

    def pair(p):
        return slice(p * 2 * cc, (p + 1) * 2 * cc)

    def in_dot(col0):
        return matmul(h_bf[...], win_ref[:, col0:col0 + 2 * cc])

    def cvg_dot(c):
        res = in_dot(c * 2 * cc)
        cvg[c % 2, 0] = res[:, :, 0:cc]
        cvg[c % 2, 1] = res[:, :, cc:]

    def cs_dot(p):
        cs_scr[:, :, pair(p)] = in_dot(2 * d_conv + p * 2 * cc)

    def lx_dot(p):
        lx_scr[halo_l:halo_l + tc, :, pair(p)] = in_dot(3 * d_conv + p * 2 * cc)

    def ls_dot(p):
        ls_scr[:, :, pair(p)] = in_dot(3 * d_conv + d_lru + p * 2 * cc)

    def glu(c, tb):
        t0 = tb * TIME_BLOCK
        tt = slice(t0, t0 + TIME_BLOCK)
        hv = cvg[c % 2, 0, tt]
        v_scr[halo_v + t0:halo_v + t0 + TIME_BLOCK, :, chunk(c)] = (
            hv * jnp.tanh(cvg[c % 2, 1, tt]) + hv)

    def conv(tb, s):
        t0 = tb * TIME_BLOCK
        sl = strip(s)
        acc = dww_ref[0:1, sl] * v_scr[t0:t0 + TIME_BLOCK, :, sl]
        for j in range(1, k_conv):
            acc = acc + dww_ref[j:j + 1, sl] * v_scr[t0 + j:t0 + j + TIME_BLOCK, :, sl]
        conv_scr[t0:t0 + TIME_BLOCK, :, sl] = acc + dwb_ref[:, sl]

    def short_conv(tb, s):
        t0 = tb * TIME_BLOCK
        sl = strip(s)
        acc = scw_ref[0:1, sl] * lx_scr[t0:t0 + TIME_BLOCK, :, sl]
        for j in range(1, k_short):
            acc = acc + scw_ref[j:j + 1, sl] * lx_scr[t0 + j:t0 + j + TIME_BLOCK, :, sl]
        hx = acc + scb_ref[:, sl]
        xs_scr[t0:t0 + TIME_BLOCK, :, sl] = hx
        xs_bf[rows(t0, TIME_BLOCK), sl] = (
            hx.reshape(TIME_BLOCK * bsz, LANES).astype(BF16))

    def gate_dot(hd):
        c0 = hd * head_dim
        res = matmul(xs_bf[:, c0:c0 + head_dim], wri_ref[hd])
        ri_scr[:, :, c0:c0 + head_dim] = res[:, :, 0:head_dim]
        ri_scr[:, :, d_lru + c0:d_lru + c0 + head_dim] = res[:, :, head_dim:]

    def layer_norm(t):
        v = conv_scr[t]
        mu = jnp.mean(v, axis=-1, keepdims=True)
        dv = v - mu
        var = jnp.mean(dv * dv, axis=-1, keepdims=True)
        hy = (dv * lax.rsqrt(var + EPS)) * clng_ref[...] + clnb_ref[...]
        ln_bf[rows(t), :] = _half_silu(hy).astype(BF16)

    def pw_dot(p):
        t1[:, :, pair(p)] = matmul(ln_bf[...], pw2w_ref[:, pair(p)])

    def conv_out(t):
        y = (t1[t] + pw2b_ref[...]) * _half_silu(cs_scr[t])
        ycat[rows(t), 0:d_conv] = y.astype(BF16)

    state = [None]

    def scan(t):
        ri = ri_scr[t]
        tr = jnp.tanh(ri[:, 0:d_lru] + br_ref[...])
        ti = jnp.tanh(ri[:, d_lru:] + bi_ref[...])
        hsp = sp_scr[...]
        a = jnp.exp(tr * hsp + hsp)
        mult = jnp.sqrt(1.0 - a * a)
        hx = xs_scr[t]
        h = a * state[0] + mult * (ti * hx + hx)
        state[0] = h
        ycat[rows(t), d_conv:] = (h * _half_silu(ls_scr[t])).astype(BF16)

    def out_dot_conv(p):
        t1[:, :, pair(p)] = matmul(ycat[:, 0:d_conv],
                                   wout_ref[0:d_conv // 2, pair(p)])

    def out_dot_lru(p):
        t1[:, :, pair(p)] += matmul(ycat[:, d_conv:],
                                    wout_ref[d_conv // 2:, pair(p)])

    def out(t):
        xo = x_at(t) + (1.0 + mod_ref[2]) * t1[t]
        if final_norm:
            ms = jnp.mean(xo * xo, axis=-1, keepdims=True)
            xo = (xo * lax.rsqrt(ms + EPS)) * fg_ref[...]
        if batch_major_out:
            o_buf[slot, t] = xo
        else:
            o_out[t] = xo

    dot_cost = 1024.0
    gate_cost = 256.0
    conv_cost = 274.0
    sc_cost = 35.0
    glu_cost = 40.0

    for t in range(tc):
        norm(t)

    lam = lam_ref[...]
    sp_scr[...] = (-0.5 * LRU_C) * (jnp.maximum(-lam, 0.0)
                                    + jnp.log1p(jnp.exp(-jnp.abs(lam))))

    cvg_dot(0)

    def dots(fn, ps):
        return [(dot_cost, functools.partial(fn, p)) for p in ps]

    def gate_dots(hds):
        return [(gate_cost, functools.partial(gate_dot, hd)) for hd in hds]

    def short_convs(strips):
        return [(sc_cost, functools.partial(short_conv, tb, s))
                for s in strips for tb in range(n_tb)]

    mxu_rounds = [
        dots(cvg_dot, [1]) + dots(lx_dot, [0]),
        dots(cvg_dot, [2]) + dots(lx_dot, [1]),
        dots(cvg_dot, [3]) + dots(cs_dot, [0]) + gate_dots(range(0, 4)),
        dots(cs_dot, [1]) + dots(ls_dot, [0, 1]) + gate_dots(range(4, 8)),
    ]
    valu_extra = [[], short_convs(range(0, 4)), short_convs(range(4, 8)), []]
    for r in range(n_chunks):
        valu = [(glu_cost, functools.partial(glu, r, tb)) for tb in range(n_tb)]
        for s in range(r * strips_per_chunk, (r + 1) * strips_per_chunk):
            valu += [(conv_cost, functools.partial(conv, tb, s)) for tb in range(n_tb)]
        _emit_interleaved(mxu_rounds[r], valu + valu_extra[r])

    for t in range(tc):
        layer_norm(t)

    state[0] = h_carry[...]
    half = tc // 2
    n_pairs = d // (2 * cc)
    _emit_interleaved(dots(pw_dot, range(n_pairs)),
                      [(90.0, functools.partial(scan, t)) for t in range(half)])
    for t in range(tc):
        conv_out(t)
    _emit_interleaved(dots(out_dot_conv, range(n_pairs)),
                      [(90.0, functools.partial(scan, t)) for t in range(half, tc)])
    h_carry[...] = state[0]
    for p in range(n_pairs):
        out_dot_lru(p)

    if batch_major_out:
        @pl.when(step >= 2)
        def _():
            for cp in out_copies(step - 2, slot):
                cp.wait()

    for t in range(tc):
        out(t)

    if batch_major_out:
        for cp in out_copies(step, slot):
            cp.start()

        @pl.when(step == n_steps - 1)
        def _():
            @pl.when(n_steps >= 2)
            def _():
                for cp in out_copies(step - 1, 1 - slot):
                    cp.wait()
            for cp in out_copies(step, slot):
                cp.wait()

    v_scr[0:halo_v] = v_scr[tc:tc + halo_v]
    lx_scr[0:halo_l] = lx_scr[tc:tc + halo_l]


def _resident(shape):
    zeros = (0,) * len(shape)
    return pl.BlockSpec(shape, lambda j: zeros, pipeline_mode=pl.Buffered(1))


def _layer(x, mod, ng, win, dww, dwb, clng, clnb, pw2w, pw2b, scw, scb, wri,
           br, bi, lam, wout, fg, *, final_norm, batch_major_in, batch_major_out):
    if batch_major_in:
        bsz, seq, d = x.shape
    else:
        seq, bsz, d = x.shape
    d_conv = dww.shape[1]
    d_lru = lam.shape[1]
    tc = TIME_CHUNK
    assert seq % tc == 0 and tc % (2 * TIME_BLOCK) == 0
    assert tc >= dww.shape[0] - 1 and tc >= scw.shape[0] - 1
    assert bsz % (2 * SUBLANES) == 0
    assert d % MXU_COLS == 0 and d_conv == d_lru and d_conv // MXU_COLS == 4
    assert wri.shape[0] == 8 and d_lru // LANES == 8
    m = tc * bsz
    operands = (mod, ng, win, dww, dwb, clng, clnb, pw2w, pw2b, scw, scb, wri,
                br, bi, lam, wout, fg)
    time_major_block = pl.BlockSpec((tc, bsz, d), lambda j: (j, 0, 0))
    hbm = pl.BlockSpec(memory_space=pl.ANY)
    io_buffers = [pltpu.VMEM((2, tc, bsz, d), x.dtype),
                  pltpu.SemaphoreType.DMA((2, bsz))]
    out_dims = (bsz, seq, d) if batch_major_out else (seq, bsz, d)
    return pl.pallas_call(
        functools.partial(_layer_kernel, final_norm=final_norm,
                          batch_major_in=batch_major_in,
                          batch_major_out=batch_major_out),
        grid=(seq // tc,),
        in_specs=[hbm if batch_major_in else time_major_block]
        + [_resident(a.shape) for a in operands],
        out_specs=hbm if batch_major_out else time_major_block,
        out_shape=jax.ShapeDtypeStruct(out_dims, x.dtype),
        scratch_shapes=[
            pltpu.VMEM((m, d), BF16),
            pltpu.VMEM((m, d_conv), BF16),
            pltpu.VMEM((m, d_lru), BF16),
            pltpu.VMEM((2, 2, tc, bsz, MXU_COLS), F32),
            pltpu.VMEM((tc, bsz, d_conv), F32),
            pltpu.VMEM((tc, bsz, d_lru), F32),
            pltpu.VMEM((tc, bsz, d), F32),
            pltpu.VMEM((tc, bsz, d_conv), F32),
            pltpu.VMEM((tc, bsz, d_lru), F32),
            pltpu.VMEM((tc + dww.shape[0] - 1, bsz, d_conv), F32),
            pltpu.VMEM((tc + scw.shape[0] - 1, bsz, d_lru), F32),
            pltpu.VMEM((tc, bsz, 2 * d_lru), F32),
            pltpu.VMEM((m, d_conv + d_lru), BF16),
            pltpu.VMEM((1, d_lru), F32),
            pltpu.VMEM((bsz, d_lru), F32),
        ] + (io_buffers if batch_major_in else [])
        + (io_buffers if batch_major_out else []),
        compiler_params=pltpu.CompilerParams(
            dimension_semantics=("arbitrary",),
            vmem_limit_bytes=VMEM_LIMIT_BYTES),
        name="final_layer" if final_norm else "layer",
    )(x, *operands)


def _pack_bf16_rows(w):
    bits = lax.bitcast_convert_type(w.astype(BF16), jnp.uint16).astype(jnp.uint32)
    return bits[..., 0::2, :] | (bits[..., 1::2, :] << 16)


def _prepare_w_in(w, d_conv, d_lru):
    k = w.shape[0]
    n_cv = d_conv // MXU_COLS
    cv = 0.5 * w[:, 0:d_conv].reshape(k, n_cv, MXU_COLS)
    cg = 0.5 * w[:, d_conv:2 * d_conv].reshape(k, n_cv, MXU_COLS)
    head = jnp.stack([cv, cg], axis=2).reshape(k, 2 * d_conv)
    return jnp.concatenate(
        [head, 0.5 * w[:, 2 * d_conv:3 * d_conv],
         w[:, 3 * d_conv:3 * d_conv + d_lru],
         0.5 * w[:, 3 * d_conv + d_lru:]], axis=1)


def kernel(x, c, norm_g, mod_w, mod_b, w_in, dw_w, dw_b, cln_g, cln_b, pw2_w,
           pw2_b, sc_w, sc_b, wr, br, wi, bi, lam, w_out, final_g):
    depth = norm_g.shape[0]
    d_conv = dw_w.shape[2]
    d_lru = lam.shape[1]
    mod = _modulation(c, mod_w, mod_b)
    row = lambda a: a.reshape(1, -1)
    fg = row(final_g)
    h = x
    for l in range(depth):
        wri = jnp.concatenate([wr[l], wi[l]], axis=-1)
        h = _layer(
            h, mod[l], row(norm_g[l]),
            _pack_bf16_rows(_prepare_w_in(w_in[l], d_conv, d_lru)), dw_w[l],
            row(dw_b[l]), row(0.5 * cln_g[l]), row(0.5 * cln_b[l]),
            _pack_bf16_rows(pw2_w[l]),
            row(pw2_b[l]), 0.5 * sc_w[l], row(0.5 * sc_b[l]), _pack_bf16_rows(wri),
            row(0.5 * br[l]), row(0.5 * bi[l]), row(lam[l]),
            _pack_bf16_rows(w_out[l]), fg, final_norm=(l == depth - 1),
            batch_major_in=(l == 0), batch_major_out=(l == depth - 1))
    return h
```

```python
import functools

import jax
import jax.numpy as jnp
from jax import lax
from jax.experimental import pallas as pl
from jax.experimental.pallas import tpu as pltpu

EPS = 1e-6
LRU_C = 8.0
LANES = 128
SUBLANES = 8
MXU_COLS = 256
TIME_CHUNK = 32
TIME_BLOCK = 8
VMEM_LIMIT_BYTES = 58 * 1024 * 1024

F32 = jnp.float32
BF16 = jnp.bfloat16


def _half_silu(h):
    return h * jnp.tanh(h) + h


def _emit_interleaved(mxu_tasks, valu_tasks):
    ia = ib = 0
    ca = cb = 0.0
    while ia < len(mxu_tasks) or ib < len(valu_tasks):
        if ib >= len(valu_tasks) or (ia < len(mxu_tasks) and ca <= cb):
            cost, fn = mxu_tasks[ia]
            ia += 1
            ca += cost
        else:
            cost, fn = valu_tasks[ib]
            ib += 1
            cb += cost
        fn()


def _mod_kernel(c_ref, w_ref, b_ref, o_ref):
    c = c_ref[...]
    o_ref[0, 0] = jnp.dot(_half_silu(0.5 * c), w_ref[0], preferred_element_type=F32,
                          precision=lax.Precision.HIGHEST) + b_ref[0]


def _modulation(c, mod_w, mod_b):
    depth, d, d3 = mod_w.shape
    bsz = c.shape[0]
    n3 = d3 // d
    return pl.pallas_call(
        _mod_kernel,
        grid=(depth, n3),
        in_specs=[
            pl.BlockSpec((bsz, d), lambda l, j: (0, 0)),
            pl.BlockSpec((1, d, d), lambda l, j: (l, 0, j)),
            pl.BlockSpec((1, 1, d), lambda l, j: (l, 0, j)),
        ],
        out_specs=pl.BlockSpec((1, 1, bsz, d), lambda l, j: (l, j, 0, 0)),
        out_shape=jax.ShapeDtypeStruct((depth, n3, bsz, d), F32),
        name="modulation",
    )(c, mod_w, mod_b.reshape(depth, 1, d3))


def _layer_kernel(x_in, mod_ref, ng_ref, win_ref, dww_ref, dwb_ref, clng_ref,
                  clnb_ref, pw2w_ref, pw2b_ref, scw_ref, scb_ref, wri_ref,
                  br_ref, bi_ref, lam_ref, wout_ref, fg_ref,
                  o_out,
                  h_bf, ln_bf, xs_bf, cvg, conv_scr, xs_scr, t1, cs_scr, ls_scr,
                  v_scr, lx_scr, ri_scr, ycat, sp_scr, h_carry, *io_scratch,
                  final_norm, batch_major_in, batch_major_out):
    tc, bsz, d = t1.shape
    d_conv = dww_ref.shape[1]
    d_lru = lam_ref.shape[1]
    k_conv = dww_ref.shape[0]
    k_short = scw_ref.shape[0]
    halo_v = k_conv - 1
    halo_l = k_short - 1
    n_heads = wri_ref.shape[0]
    head_dim = d_lru // n_heads
    cc = MXU_COLS
    n_chunks = d_conv // cc
    strips_per_chunk = cc // LANES
    n_tb = tc // TIME_BLOCK
    step = pl.program_id(0)
    n_steps = pl.num_programs(0)
    slot = lax.rem(step, 2)

    io_scratch = list(io_scratch)
    if batch_major_in:
        x_buf, in_sem = io_scratch[:2]
        io_scratch = io_scratch[2:]
    if batch_major_out:
        o_buf, out_sem = io_scratch[:2]

    @pl.when(step == 0)
    def _():
        v_scr[0:halo_v] = jnp.zeros((halo_v, bsz, d_conv), F32)
        lx_scr[0:halo_l] = jnp.zeros((halo_l, bsz, d_lru), F32)
        h_carry[...] = jnp.zeros((bsz, d_lru), F32)

    def in_copies(s, buf_slot):
        return [pltpu.make_async_copy(x_in.at[b, pl.ds(s * tc, tc), :],
                                      x_buf.at[buf_slot, :, b, :],
                                      in_sem.at[buf_slot, b]) for b in range(bsz)]

    def out_copies(s, buf_slot):
        return [pltpu.make_async_copy(o_buf.at[buf_slot, :, b, :],
                                      o_out.at[b, pl.ds(s * tc, tc), :],
                                      out_sem.at[buf_slot, b]) for b in range(bsz)]

    if batch_major_in:
        @pl.when(step == 0)
        def _():
            for cp in in_copies(0, 0):
                cp.start()

        @pl.when(step + 1 < n_steps)
        def _():
            for cp in in_copies(step + 1, 1 - slot):
                cp.start()

        for cp in in_copies(step, slot):
            cp.wait()

    def x_at(t):
        return x_buf[slot, t] if batch_major_in else x_in[t]

    def rows(t, n=1):
        return slice(t * bsz, (t + n) * bsz)

    def chunk(c):
        return slice(c * cc, (c + 1) * cc)

    def strip(s):
        return slice(s * LANES, (s + 1) * LANES)

    def matmul(lhs, packed_w):
        res = jnp.dot(lhs, pltpu.bitcast(packed_w, BF16),
                      preferred_element_type=F32)
        return res.reshape(tc, bsz, res.shape[-1])

    def norm(t):
        x = x_at(t)
        ms = jnp.mean(x * x, axis=-1, keepdims=True)
        xn = (x * lax.rsqrt(ms + EPS)) * ng_ref[...]
        h = xn * (1.0 + mod_ref[1]) + mod_ref[0]
        h_bf[rows(t), :] = h.astype(BF16)

    def pair(p):
        return slice(p * 2 * cc, (p + 1) * 2 * cc)

    def in_dot(col0):
        return matmul(h_bf[...], win_ref[:, col0:col0 + 2 * cc])

    def cvg_dot(c):
        res = in_dot(c * 2 * cc)
        cvg[c % 2, 0] = res[:, :, 0:cc]
        cvg[c % 2, 1] = res[:, :, cc:]

    def cs_dot(p):
        cs_scr[:, :, pair(p)] = in_dot(2 * d_conv + p * 2 * cc)

    def lx_dot(p):
        lx_scr[halo_l:halo_l + tc, :, pair(p)] = in_dot(3 * d_conv + p * 2 * cc)

    def ls_dot(p):
        ls_scr[:, :, pair(p)] = in_dot(3 * d_conv + d_lru + p * 2 * cc)

    def glu(c, tb):
        t0 = tb * TIME_BLOCK
        tt = slice(t0, t0 + TIME_BLOCK)
        hv = cvg[c % 2, 0, tt]
        v_scr[halo_v + t0:halo_v + t0 + TIME_BLOCK, :, chunk(c)] = (
            hv * jnp.tanh(cvg[c % 2, 1, tt]) + hv)

    def conv(tb, s):
        t0 = tb * TIME_BLOCK
        sl = strip(s)
        acc = dww_ref[0:1, sl] * v_scr[t0:t0 + TIME_BLOCK, :, sl]
        for j in range(1, k_conv):
            acc = acc + dww_ref[j:j + 1, sl] * v_scr[t0 + j:t0 + j + TIME_BLOCK, :, sl]
        conv_scr[t0:t0 + TIME_BLOCK, :, sl] = acc + dwb_ref[:, sl]

    def short_conv(tb, s):
        t0 = tb * TIME_BLOCK
        sl = strip(s)
        acc = scw_ref[0:1, sl] * lx_scr[t0:t0 + TIME_BLOCK, :, sl]
        for j in range(1, k_short):
            acc = acc + scw_ref[j:j + 1, sl] * lx_scr[t0 + j:t0 + j + TIME_BLOCK, :, sl]
        hx = acc + scb_ref[:, sl]
        xs_scr[t0:t0 + TIME_BLOCK, :, sl] = hx
        xs_bf[rows(t0, TIME_BLOCK), sl] = (
            hx.reshape(TIME_BLOCK * bsz, LANES).astype(BF16))

    def gate_dot(hd):
        c0 = hd * head_dim
        res = matmul(xs_bf[:, c0:c0 + head_dim], wri_ref[hd])
        ri_scr[:, :, c0:c0 + head_dim] = res[:, :, 0:head_dim]
        ri_scr[:, :, d_lru + c0:d_lru + c0 + head_dim] = res[:, :, head_dim:]

    def layer_norm(t):
        v = conv_scr[t]
        mu = jnp.mean(v, axis=-1, keepdims=True)
        dv = v - mu
        var = jnp.mean(dv * dv, axis=-1, keepdims=True)
        hy = (dv * lax.rsqrt(var + EPS)) * clng_ref[...] + clnb_ref[...]
        ln_bf[rows(t), :] = _half_silu(hy).astype(BF16)

    def pw_dot(p):
        t1[:, :, pair(p)] = matmul(ln_bf[...], pw2w_ref[:, pair(p)])

    def conv_out(t):
        y = (t1[t] + pw2b_ref[...]) * _half_silu(cs_scr[t])
        ycat[rows(t), 0:d_conv] = y.astype(BF16)

    state = [None]

    def scan(t):
        ri = ri_scr[t]
        tr = jnp.tanh(ri[:, 0:d_lru] + br_ref[...])
        ti = jnp.tanh(ri[:, d_lru:] + bi_ref[...])
        hsp = sp_scr[...]
        a = jnp.exp(tr * hsp + hsp)
        mult = jnp.sqrt(1.0 - a * a)
        hx = xs_scr[t]
        h = a * state[0] + mult * (ti * hx + hx)
        state[0] = h
        ycat[rows(t), d_conv:] = (h * _half_silu(ls_scr[t])).astype(BF16)

    def out_dot_conv(p):
        t1[:, :, pair(p)] = matmul(ycat[:, 0:d_conv],
                                   wout_ref[0:d_conv // 2, pair(p)])

    def out_dot_lru(p):
        t1[:, :, pair(p)] += matmul(ycat[:, d_conv:],
                                    wout_ref[d_conv // 2:, pair(p)])

    def out(t):
        xo = x_at(t) + (1.0 + mod_ref[2]) * t1[t]
        if final_norm:
            ms = jnp.mean(xo * xo, axis=-1, keepdims=True)
            xo = (xo * lax.rsqrt(ms + EPS)) * fg_ref[...]
        if batch_major_out:
            o_buf[slot, t] = xo
        else:
            o_out[t] = xo

    dot_cost = 1024.0
    gate_cost = 256.0
    conv_cost = 274.0
    sc_cost = 35.0
    glu_cost = 40.0

    for t in range(tc):
        norm(t)

    lam = lam_ref[...]
    sp_scr[...] = (-0.5 * LRU_C) * (jnp.maximum(-lam, 0.0)
                                    + jnp.log1p(jnp.exp(-jnp.abs(lam))))

    cvg_dot(0)

    def dots(fn, ps):
        return [(dot_cost, functools.partial(fn, p)) for p in ps]

    def gate_dots(hds):
        return [(gate_cost, functools.partial(gate_dot, hd)) for hd in hds]

    def short_convs(strips):
        return [(sc_cost, functools.partial(short_conv, tb, s))
                for s in strips for tb in range(n_tb)]

    mxu_rounds = [
        dots(cvg_dot, [1]) + dots(lx_dot, [0]),
        dots(cvg_dot, [2]) + dots(lx_dot, [1]),
        dots(cvg_dot, [3]) + dots(cs_dot, [0]) + gate_dots(range(0, 4)),
        dots(cs_dot, [1]) + dots(ls_dot, [0, 1]) + gate_dots(range(4, 8)),
    ]
    valu_extra = [[], short_convs(range(0, 4)), short_convs(range(4, 8)), []]
    for r in range(n_chunks):
        valu = [(glu_cost, functools.partial(glu, r, tb)) for tb in range(n_tb)]
        for s in range(r * strips_per_chunk, (r + 1) * strips_per_chunk):
            valu += [(conv_cost, functools.partial(conv, tb, s)) for tb in range(n_tb)]
        _emit_interleaved(mxu_rounds[r], valu + valu_extra[r])

    for t in range(tc):
        layer_norm(t)

    state[0] = h_carry[...]
    half = tc // 2
    n_pairs = d // (2 * cc)
    _emit_interleaved(dots(pw_dot, range(n_pairs)),
                      [(90.0, functools.partial(scan, t)) for t in range(half)])
    for t in range(tc):
        conv_out(t)
    _emit_interleaved(dots(out_dot_conv, range(n_pairs)),
                      [(90.0, functools.partial(scan, t)) for t in range(half, tc)])
    h_carry[...] = state[0]
    for p in range(n_pairs):
        out_dot_lru(p)

    if batch_major_out:
        @pl.when(step >= 2)
        def _():
            for cp in out_copies(step - 2, slot):
                cp.wait()

    for t in range(tc):
        out(t)

    if batch_major_out:
        for cp in out_copies(step, slot):
            cp.start()

        @pl.when(step == n_steps - 1)
        def _():
            @pl.when(n_steps >= 2)
            def _():
                for cp in out_copies(step - 1, 1 - slot):
                    cp.wait()
            for cp in out_copies(step, slot):
                cp.wait()

    v_scr[0:halo_v] = v_scr[tc:tc + halo_v]
    lx_scr[0:halo_l] = lx_scr[tc:tc + halo_l]


def _resident(shape):
    zeros = (0,) * len(shape)
    return pl.BlockSpec(shape, lambda j: zeros, pipeline_mode=pl.Buffered(1))


def _layer(x, mod, ng, win, dww, dwb, clng, clnb, pw2w, pw2b, scw, scb, wri,
           br, bi, lam, wout, fg, *, final_norm, batch_major_in, batch_major_out):
    if batch_major_in:
        bsz, seq, d = x.shape
    else:
        seq, bsz, d = x.shape
    d_conv = dww.shape[1]
    d_lru = lam.shape[1]
    tc = TIME_CHUNK
    assert seq % tc == 0 and tc % (2 * TIME_BLOCK) == 0
    assert tc >= dww.shape[0] - 1 and tc >= scw.shape[0] - 1
    assert bsz % (2 * SUBLANES) == 0
    assert d % MXU_COLS == 0 and d_conv == d_lru and d_conv // MXU_COLS == 4
    assert wri.shape[0] == 8 and d_lru // LANES == 8
    m = tc * bsz
    operands = (mod, ng, win, dww, dwb, clng, clnb, pw2w, pw2b, scw, scb, wri,
                br, bi, lam, wout, fg)
    time_major_block = pl.BlockSpec((tc, bsz, d), lambda j: (j, 0, 0))
    hbm = pl.BlockSpec(memory_space=pl.ANY)
    io_buffers = [pltpu.VMEM((2, tc, bsz, d), x.dtype),
                  pltpu.SemaphoreType.DMA((2, bsz))]
    out_dims = (bsz, seq, d) if batch_major_out else (seq, bsz, d)
    return pl.pallas_call(
        functools.partial(_layer_kernel, final_norm=final_norm,
                          batch_major_in=batch_major_in,
                          batch_major_out=batch_major_out),
        grid=(seq // tc,),
        in_specs=[hbm if batch_major_in else time_major_block]
        + [_resident(a.shape) for a in operands],
        out_specs=hbm if batch_major_out else time_major_block,
        out_shape=jax.ShapeDtypeStruct(out_dims, x.dtype),
        scratch_shapes=[
            pltpu.VMEM((m, d), BF16),
            pltpu.VMEM((m, d_conv), BF16),
            pltpu.VMEM((m, d_lru), BF16),
            pltpu.VMEM((2, 2, tc, bsz, MXU_COLS), F32),
            pltpu.VMEM((tc, bsz, d_conv), F32),
            pltpu.VMEM((tc, bsz, d_lru), F32),
            pltpu.VMEM((tc, bsz, d), F32),
            pltpu.VMEM((tc, bsz, d_conv), F32),
            pltpu.VMEM((tc, bsz, d_lru), F32),
            pltpu.VMEM((tc + dww.shape[0] - 1, bsz, d_conv), F32),
            pltpu.VMEM((tc + scw.shape[0] - 1, bsz, d_lru), F32),
            pltpu.VMEM((tc, bsz, 2 * d_lru), F32),
            pltpu.VMEM((m, d_conv + d_lru), BF16),
            pltpu.VMEM((1, d_lru), F32),
            pltpu.VMEM((bsz, d_lru), F32),
        ] + (io_buffers if batch_major_in else [])
        + (io_buffers if batch_major_out else []),
        compiler_params=pltpu.CompilerParams(
            dimension_semantics=("arbitrary",),
            vmem_limit_bytes=VMEM_LIMIT_BYTES),
        name="final_layer" if final_norm else "layer",
    )(x, *operands)


def _pack_bf16_rows(w):
    *lead, k, n = w.shape
    pairs = w.reshape(*lead, k // 2, 2 * n)
    bits = lax.bitcast_convert_type(pairs.astype(BF16), jnp.uint16).astype(jnp.uint32)
    return bits[..., :n] | (bits[..., n:] << 16)


def _prepare_w_in(w, d_conv, d_lru):
    cc = MXU_COLS
    head = []
    for c in range(d_conv // cc):
        head += [w[:, c * cc:(c + 1) * cc], w[:, d_conv + c * cc:d_conv + (c + 1) * cc]]
    return jnp.concatenate(
        [0.5 * h for h in head]
        + [0.5 * w[:, 2 * d_conv:3 * d_conv],
           w[:, 3 * d_conv:3 * d_conv + d_lru],
           0.5 * w[:, 3 * d_conv + d_lru:]], axis=1)


def kernel(x, c, norm_g, mod_w, mod_b, w_in, dw_w, dw_b, cln_g, cln_b, pw2_w,
           pw2_b, sc_w, sc_b, wr, br, wi, bi, lam, w_out, final_g):
    depth = norm_g.shape[0]
    d_conv = dw_w.shape[2]
    d_lru = lam.shape[1]
    mod = _modulation(c, mod_w, mod_b)
    row = lambda a: a.reshape(1, -1)
    fg = row(final_g)
    h = x
    for l in range(depth):
        wri = jnp.concatenate([wr[l], wi[l]], axis=-1)
        h = _layer(
            h, mod[l], row(norm_g[l]),
            _pack_bf16_rows(_prepare_w_in(w_in[l], d_conv, d_lru)), dw_w[l],
            row(dw_b[l]), row(0.5 * cln_g[l]), row(0.5 * cln_b[l]),
            _pack_bf16_rows(pw2_w[l]),
            row(pw2_b[l]), 0.5 * sc_w[l], row(0.5 * sc_b[l]), _pack_bf16_rows(wri),
            row(0.5 * br[l]), row(0.5 * bi[l]), row(lam[l]),
            _pack_bf16_rows(w_out[l]), fg, final_norm=(l == depth - 1),
            batch_major_in=(l == 0), batch_major_out=(l == depth - 1))
    return h
```

```python
import functools

import jax
import jax.numpy as jnp
from jax import lax
from jax.experimental import pallas as pl
from jax.experimental.pallas import tpu as pltpu

EPS = 1e-6
LRU_C = 8.0
LANES = 128
SUBLANES = 8
MXU_COLS = 256
TIME_CHUNK = 32
TIME_BLOCK = 8
VMEM_LIMIT_BYTES = 58 * 1024 * 1024

F32 = jnp.float32
BF16 = jnp.bfloat16


def _half_silu(h):
    return h * jnp.tanh(h) + h


def _emit_interleaved(mxu_tasks, valu_tasks):
    ia = ib = 0
    ca = cb = 0.0
    while ia < len(mxu_tasks) or ib < len(valu_tasks):
        if ib >= len(valu_tasks) or (ia < len(mxu_tasks) and ca <= cb):
            cost, fn = mxu_tasks[ia]
            ia += 1
            ca += cost
        else:
            cost, fn = valu_tasks[ib]
            ib += 1
            cb += cost
        fn()


def _mod_kernel(c_ref, w_ref, b_ref, o_ref):
    c = c_ref[...]
    o_ref[0, 0] = jnp.dot(_half_silu(0.5 * c), w_ref[0], preferred_element_type=F32,
                          precision=lax.Precision.HIGHEST) + b_ref[0]


def _modulation(c, mod_w, mod_b):
    depth, d, d3 = mod_w.shape
    bsz = c.shape[0]
    n3 = d3 // d
    return pl.pallas_call(
        _mod_kernel,
        grid=(depth, n3),
        in_specs=[
            pl.BlockSpec((bsz, d), lambda l, j: (0, 0)),
            pl.BlockSpec((1, d, d), lambda l, j: (l, 0, j)),
            pl.BlockSpec((1, 1, d), lambda l, j: (l, 0, j)),
        ],
        out_specs=pl.BlockSpec((1, 1, bsz, d), lambda l, j: (l, j, 0, 0)),
        out_shape=jax.ShapeDtypeStruct((depth, n3, bsz, d), F32),
        name="modulation",
    )(c, mod_w, mod_b.reshape(depth, 1, d3))


def _layer_kernel(x_in, mod_ref, ng_ref, win_ref, dww_ref, dwb_ref, clng_ref,
                  clnb_ref, pw2w_ref, pw2b_ref, scw_ref, scb_ref, wri_ref,
                  br_ref, bi_ref, lam_ref, wout_ref, fg_ref,
                  o_out,
                  h_bf, ln_bf, xs_bf, cvg, conv_scr, xs_scr, t1, cs_scr, ls_scr,
                  v_scr, lx_scr, ri_scr, ycat, sp_scr, h_carry, *io_scratch,
                  final_norm, batch_major_in, batch_major_out):
    tc, bsz, d = t1.shape
    d_conv = dww_ref.shape[1]
    d_lru = lam_ref.shape[1]
    k_conv = dww_ref.shape[0]
    k_short = scw_ref.shape[0]
    halo_v = k_conv - 1
    halo_l = k_short - 1
    head_dim = wri_ref.shape[0] * 2
    n_heads = d_lru // head_dim
    cc = MXU_COLS
    n_chunks = d_conv // cc
    strips_per_chunk = cc // LANES
    n_tb = tc // TIME_BLOCK
    step = pl.program_id(0)
    n_steps = pl.num_programs(0)
    slot = lax.rem(step, 2)

    io_scratch = list(io_scratch)
    if batch_major_in:
        x_buf, in_sem = io_scratch[:2]
        io_scratch = io_scratch[2:]
    if batch_major_out:
        o_buf, out_sem = io_scratch[:2]

    @pl.when(step == 0)
    def _():
        v_scr[0:halo_v] = jnp.zeros((halo_v, bsz, d_conv), F32)
        lx_scr[0:halo_l] = jnp.zeros((halo_l, bsz, d_lru), F32)
        h_carry[...] = jnp.zeros((bsz, d_lru), F32)

    def in_copies(s, buf_slot):
        return [pltpu.make_async_copy(x_in.at[b, pl.ds(s * tc, tc), :],
                                      x_buf.at[buf_slot, :, b, :],
                                      in_sem.at[buf_slot, b]) for b in range(bsz)]

    def out_copies(s, buf_slot):
        return [pltpu.make_async_copy(o_buf.at[buf_slot, :, b, :],
                                      o_out.at[b, pl.ds(s * tc, tc), :],
                                      out_sem.at[buf_slot, b]) for b in range(bsz)]

    if batch_major_in:
        @pl.when(step == 0)
        def _():
            for cp in in_copies(0, 0):
                cp.start()

        @pl.when(step + 1 < n_steps)
        def _():
            for cp in in_copies(step + 1, 1 - slot):
                cp.start()

        for cp in in_copies(step, slot):
            cp.wait()

    def x_at(t):
        return x_buf[slot, t] if batch_major_in else x_in[t]

    def rows(t, n=1):
        return slice(t * bsz, (t + n) * bsz)

    def chunk(c):
        return slice(c * cc, (c + 1) * cc)

    def strip(s):
        return slice(s * LANES, (s + 1) * LANES)

    def matmul(lhs, packed_w):
        res = jnp.dot(lhs, pltpu.bitcast(packed_w, BF16),
                      preferred_element_type=F32)
        return res.reshape(tc, bsz, res.shape[-1])

    def norm(t):
        x = x_at(t)
        ms = jnp.mean(x * x, axis=-1, keepdims=True)
        xn = (x * lax.rsqrt(ms + EPS)) * ng_ref[...]
        h = xn * (1.0 + mod_ref[1]) + mod_ref[0]
        h_bf[rows(t), :] = h.astype(BF16)

    def pair(p):
        return slice(p * 2 * cc, (p + 1) * 2 * cc)

    def in_dot(col0):
        return matmul(h_bf[...], win_ref[:, col0:col0 + 2 * cc])

    def cvg_dot(c):
        res = in_dot(c * 2 * cc)
        cvg[c % 2, 0] = res[:, :, 0:cc]
        cvg[c % 2, 1] = res[:, :, cc:]

    def cs_dot(p):
        cs_scr[:, :, pair(p)] = in_dot(2 * d_conv + p * 2 * cc)

    def lx_dot(p):
        lx_scr[halo_l:halo_l + tc, :, pair(p)] = in_dot(3 * d_conv + p * 2 * cc)

    def ls_dot(p):
        ls_scr[:, :, pair(p)] = in_dot(3 * d_conv + d_lru + p * 2 * cc)

    def glu(c, tb):
        t0 = tb * TIME_BLOCK
        tt = slice(t0, t0 + TIME_BLOCK)
        hv = cvg[c % 2, 0, tt]
        v_scr[halo_v + t0:halo_v + t0 + TIME_BLOCK, :, chunk(c)] = (
            hv * jnp.tanh(cvg[c % 2, 1, tt]) + hv)

    def conv(tb, s):
        t0 = tb * TIME_BLOCK
        sl = strip(s)
        acc = dww_ref[0:1, sl] * v_scr[t0:t0 + TIME_BLOCK, :, sl]
        for j in range(1, k_conv):
            acc = acc + dww_ref[j:j + 1, sl] * v_scr[t0 + j:t0 + j + TIME_BLOCK, :, sl]
        conv_scr[t0:t0 + TIME_BLOCK, :, sl] = acc + dwb_ref[:, sl]

    def short_conv(tb, s):
        t0 = tb * TIME_BLOCK
        sl = strip(s)
        acc = scw_ref[0:1, sl] * lx_scr[t0:t0 + TIME_BLOCK, :, sl]
        for j in range(1, k_short):
            acc = acc + scw_ref[j:j + 1, sl] * lx_scr[t0 + j:t0 + j + TIME_BLOCK, :, sl]
        hx = acc + scb_ref[:, sl]
        xs_scr[t0:t0 + TIME_BLOCK, :, sl] = hx
        xs_bf[rows(t0, TIME_BLOCK), sl] = (
            hx.reshape(TIME_BLOCK * bsz, LANES).astype(BF16))

    def gate_dot(hd):
        c0 = hd * head_dim
        res = matmul(xs_bf[:, c0:c0 + head_dim],
                     wri_ref[:, 2 * c0:2 * (c0 + head_dim)])
        ri_scr[:, :, c0:c0 + head_dim] = res[:, :, 0:head_dim]
        ri_scr[:, :, d_lru + c0:d_lru + c0 + head_dim] = res[:, :, head_dim:]

    def layer_norm(t):
        v = conv_scr[t]
        mu = jnp.mean(v, axis=-1, keepdims=True)
        dv = v - mu
        var = jnp.mean(dv * dv, axis=-1, keepdims=True)
        hy = (dv * lax.rsqrt(var + EPS)) * clng_ref[...] + clnb_ref[...]
        ln_bf[rows(t), :] = _half_silu(hy).astype(BF16)

    def pw_dot(p):
        t1[:, :, pair(p)] = matmul(ln_bf[...], pw2w_ref[:, pair(p)])

    def conv_out(t):
        y = (t1[t] + pw2b_ref[...]) * _half_silu(cs_scr[t])
        ycat[rows(t), 0:d_conv] = y.astype(BF16)

    state = [None]

    def scan(t):
        ri = ri_scr[t]
        tr = jnp.tanh(ri[:, 0:d_lru] + br_ref[...])
        ti = jnp.tanh(ri[:, d_lru:] + bi_ref[...])
        hsp = sp_scr[...]
        a = jnp.exp(tr * hsp + hsp)
        mult = jnp.sqrt(1.0 - a * a)
        hx = xs_scr[t]
        h = a * state[0] + mult * (ti * hx + hx)
        state[0] = h
        ycat[rows(t), d_conv:] = (h * _half_silu(ls_scr[t])).astype(BF16)

    def out_dot_conv(p):
        t1[:, :, pair(p)] = matmul(ycat[:, 0:d_conv],
                                   wout_ref[0:d_conv // 2, pair(p)])

    def out_dot_lru(p):
        t1[:, :, pair(p)] += matmul(ycat[:, d_conv:],
                                    wout_ref[d_conv // 2:, pair(p)])

    def out(t):
        xo = x_at(t) + (1.0 + mod_ref[2]) * t1[t]
        if final_norm:
            ms = jnp.mean(xo * xo, axis=-1, keepdims=True)
            xo = (xo * lax.rsqrt(ms + EPS)) * fg_ref[...]
        if batch_major_out:
            o_buf[slot, t] = xo
        else:
            o_out[t] = xo

    dot_cost = 1024.0
    gate_cost = 256.0
    conv_cost = 274.0
    sc_cost = 35.0
    glu_cost = 40.0

    for t in range(tc):
        norm(t)

    lam = lam_ref[...]
    sp_scr[...] = (-0.5 * LRU_C) * (jnp.maximum(-lam, 0.0)
                                    + jnp.log1p(jnp.exp(-jnp.abs(lam))))

    cvg_dot(0)

    def dots(fn, ps):
        return [(dot_cost, functools.partial(fn, p)) for p in ps]

    def gate_dots(hds):
        return [(gate_cost, functools.partial(gate_dot, hd)) for hd in hds]

    def short_convs(strips):
        return [(sc_cost, functools.partial(short_conv, tb, s))
                for s in strips for tb in range(n_tb)]

    mxu_rounds = [
        dots(cvg_dot, [1]) + dots(lx_dot, [0]),
        dots(cvg_dot, [2]) + dots(lx_dot, [1]),
        dots(cvg_dot, [3]) + dots(cs_dot, [0]) + gate_dots(range(0, 4)),
        dots(cs_dot, [1]) + dots(ls_dot, [0, 1]) + gate_dots(range(4, 8)),
    ]
    valu_extra = [[], short_convs(range(0, 4)), short_convs(range(4, 8)), []]
    for r in range(n_chunks):
        valu = [(glu_cost, functools.partial(glu, r, tb)) for tb in range(n_tb)]
        for s in range(r * strips_per_chunk, (r + 1) * strips_per_chunk):
            valu += [(conv_cost, functools.partial(conv, tb, s)) for tb in range(n_tb)]
        _emit_interleaved(mxu_rounds[r], valu + valu_extra[r])

    for t in range(tc):
        layer_norm(t)

    state[0] = h_carry[...]
    half = tc // 2
    n_pairs = d // (2 * cc)
    _emit_interleaved(dots(pw_dot, range(n_pairs)),
                      [(90.0, functools.partial(scan, t)) for t in range(half)])
    for t in range(tc):
        conv_out(t)
    _emit_interleaved(dots(out_dot_conv, range(n_pairs)),
                      [(90.0, functools.partial(scan, t)) for t in range(half, tc)])
    h_carry[...] = state[0]
    for p in range(n_pairs):
        out_dot_lru(p)

    if batch_major_out:
        @pl.when(step >= 2)
        def _():
            for cp in out_copies(step - 2, slot):
                cp.wait()

    for t in range(tc):
        out(t)

    if batch_major_out:
        for cp in out_copies(step, slot):
            cp.start()

        @pl.when(step == n_steps - 1)
        def _():
            @pl.when(n_steps >= 2)
            def _():
                for cp in out_copies(step - 1, 1 - slot):
                    cp.wait()
            for cp in out_copies(step, slot):
                cp.wait()

    v_scr[0:halo_v] = v_scr[tc:tc + halo_v]
    lx_scr[0:halo_l] = lx_scr[tc:tc + halo_l]


def _resident(shape):
    zeros = (0,) * len(shape)
    return pl.BlockSpec(shape, lambda j: zeros, pipeline_mode=pl.Buffered(1))


def _layer(x, mod, ng, win, dww, dwb, clng, clnb, pw2w, pw2b, scw, scb, wri,
           br, bi, lam, wout, fg, *, final_norm, batch_major_in, batch_major_out):
    if batch_major_in:
        bsz, seq, d = x.shape
    else:
        seq, bsz, d = x.shape
    d_conv = dww.shape[1]
    d_lru = lam.shape[1]
    tc = TIME_CHUNK
    assert seq % tc == 0 and tc % (2 * TIME_BLOCK) == 0
    assert tc >= dww.shape[0] - 1 and tc >= scw.shape[0] - 1
    assert bsz % (2 * SUBLANES) == 0
    assert d % MXU_COLS == 0 and d_conv == d_lru and d_conv // MXU_COLS == 4
    assert wri.shape == (LANES // 2, 2 * d_lru) and d_lru // LANES == 8
    m = tc * bsz
    operands = (mod, ng, win, dww, dwb, clng, clnb, pw2w, pw2b, scw, scb, wri,
                br, bi, lam, wout, fg)
    time_major_block = pl.BlockSpec((tc, bsz, d), lambda j: (j, 0, 0))
    hbm = pl.BlockSpec(memory_space=pl.ANY)
    io_buffers = [pltpu.VMEM((2, tc, bsz, d), x.dtype),
                  pltpu.SemaphoreType.DMA((2, bsz))]
    out_dims = (bsz, seq, d) if batch_major_out else (seq, bsz, d)
    return pl.pallas_call(
        functools.partial(_layer_kernel, final_norm=final_norm,
                          batch_major_in=batch_major_in,
                          batch_major_out=batch_major_out),
        grid=(seq // tc,),
        in_specs=[hbm if batch_major_in else time_major_block]
        + [_resident(a.shape) for a in operands],
        out_specs=hbm if batch_major_out else time_major_block,
        out_shape=jax.ShapeDtypeStruct(out_dims, x.dtype),
        scratch_shapes=[
            pltpu.VMEM((m, d), BF16),
            pltpu.VMEM((m, d_conv), BF16),
            pltpu.VMEM((m, d_lru), BF16),
            pltpu.VMEM((2, 2, tc, bsz, MXU_COLS), F32),
            pltpu.VMEM((tc, bsz, d_conv), F32),
            pltpu.VMEM((tc, bsz, d_lru), F32),
            pltpu.VMEM((tc, bsz, d), F32),
            pltpu.VMEM((tc, bsz, d_conv), F32),
            pltpu.VMEM((tc, bsz, d_lru), F32),
            pltpu.VMEM((tc + dww.shape[0] - 1, bsz, d_conv), F32),
            pltpu.VMEM((tc + scw.shape[0] - 1, bsz, d_lru), F32),
            pltpu.VMEM((tc, bsz, 2 * d_lru), F32),
            pltpu.VMEM((m, d_conv + d_lru), BF16),
            pltpu.VMEM((1, d_lru), F32),
            pltpu.VMEM((bsz, d_lru), F32),
        ] + (io_buffers if batch_major_in else [])
        + (io_buffers if batch_major_out else []),
        compiler_params=pltpu.CompilerParams(
            dimension_semantics=("arbitrary",),
            vmem_limit_bytes=VMEM_LIMIT_BYTES),
        name="final_layer" if final_norm else "layer",
    )(x, *operands)


def _pack_weight(w, layer, src_chunk=lambda c: c, scale_of_chunk=None):
    _, k, n = w.shape
    cc = MXU_COLS
    return pl.pallas_call(
        functools.partial(_pack_kernel, scale_of_chunk=scale_of_chunk),
        grid=(n // cc,),
        in_specs=[pl.BlockSpec((None, k, cc), lambda c: (layer, 0, src_chunk(c)))],
        out_specs=pl.BlockSpec((k // 2, cc), lambda c: (0, c)),
        out_shape=jax.ShapeDtypeStruct((k // 2, n), jnp.uint32),
        name="pack_weight",
    )(w)


def _pack_kernel(w_ref, o_ref, *, scale_of_chunk):
    w = w_ref[...]
    if scale_of_chunk is not None:
        w = w * scale_of_chunk(pl.program_id(0))
    o_ref[...] = pltpu.bitcast(w.astype(BF16), jnp.uint32)


def kernel(x, c, norm_g, mod_w, mod_b, w_in, dw_w, dw_b, cln_g, cln_b, pw2_w,
           pw2_b, sc_w, sc_b, wr, br, wi, bi, lam, w_out, final_g):
    depth = norm_g.shape[0]
    d_conv = dw_w.shape[2]
    d_lru = lam.shape[1]
    mod = _modulation(c, mod_w, mod_b)
    row = lambda a: a.reshape(1, -1)
    fg = row(final_g)
    n_cv = d_conv // MXU_COLS
    n_lx = d_lru // MXU_COLS

    def w_in_src_chunk(c):
        return jnp.where(c < 2 * n_cv, c // 2 + (c % 2) * n_cv, c)

    def w_in_scale(c):
        return jnp.where((c >= 3 * n_cv) & (c < 3 * n_cv + n_lx), 1.0, 0.5)

    n_heads, head_dim = wr.shape[1], wr.shape[2]
    wri = jnp.swapaxes(jnp.concatenate([wr, wi], axis=-1), 1, 2)
    wri = wri.reshape(depth, head_dim, n_heads * 2 * head_dim)

    h = x
    for l in range(depth):
        h = _layer(
            h, mod[l], row(norm_g[l]),
            _pack_weight(w_in, l, w_in_src_chunk, w_in_scale), dw_w[l],
            row(dw_b[l]), row(0.5 * cln_g[l]), row(0.5 * cln_b[l]),
            _pack_weight(pw2_w, l),
            row(pw2_b[l]), 0.5 * sc_w[l], row(0.5 * sc_b[l]), _pack_weight(wri, l),
            row(0.5 * br[l]), row(0.5 * bi[l]), row(lam[l]),
            _pack_weight(w_out, l), fg, final_norm=(l == depth - 1),
            batch_major_in=(l == 0), batch_major_out=(l == depth - 1))
    return h
```

```python
import functools

import jax
import jax.numpy as jnp
from jax import lax
from jax.experimental import pallas as pl
from jax.experimental.pallas import tpu as pltpu

EPS = 1e-6
LRU_C = 8.0
LANES = 128
SUBLANES = 8
MXU_COLS = 256
TIME_CHUNK = 32
TIME_BLOCK = 8
VMEM_LIMIT_BYTES = 58 * 1024 * 1024

F32 = jnp.float32
BF16 = jnp.bfloat16


def _half_silu(h):
    return h * jnp.tanh(h) + h


def _emit_interleaved(mxu_tasks, valu_tasks):
    ia = ib = 0
    ca = cb = 0.0
    while ia < len(mxu_tasks) or ib < len(valu_tasks):
        if ib >= len(valu_tasks) or (ia < len(mxu_tasks) and ca <= cb):
            cost, fn = mxu_tasks[ia]
            ia += 1
            ca += cost
        else:
            cost, fn = valu_tasks[ib]
            ib += 1
            cb += cost
        fn()


def _mod_kernel(c_ref, w_ref, b_ref, o_ref):
    c = c_ref[...]
    o_ref[0, 0] = jnp.dot(_half_silu(0.5 * c), w_ref[0], preferred_element_type=F32,
                          precision=lax.Precision.HIGHEST) + b_ref[0]


def _modulation(c, mod_w, mod_b):
    depth, d, d3 = mod_w.shape
    bsz = c.shape[0]
    n3 = d3 // d
    return pl.pallas_call(
        _mod_kernel,
        grid=(depth, n3),
        in_specs=[
            pl.BlockSpec((bsz, d), lambda l, j: (0, 0)),
            pl.BlockSpec((1, d, d), lambda l, j: (l, 0, j)),
            pl.BlockSpec((1, 1, d), lambda l, j: (l, 0, j)),
        ],
        out_specs=pl.BlockSpec((1, 1, bsz, d), lambda l, j: (l, j, 0, 0)),
        out_shape=jax.ShapeDtypeStruct((depth, n3, bsz, d), F32),
        name="modulation",
    )(c, mod_w, mod_b.reshape(depth, 1, d3))


def _layer_kernel(x_in, mod_ref, ng_ref, win_ref, dww_ref, dwb_ref, clng_ref,
                  clnb_ref, pw2w_ref, pw2b_ref, scw_ref, scb_ref, wri_ref,
                  br_ref, bi_ref, lam_ref, wout_ref, fg_ref,
                  o_out,
                  h_bf, ln_bf, xs_bf, cvg, conv_scr, xs_scr, t1, cs_scr, ls_scr,
                  v_scr, lx_scr, ri_scr, ycat, sp_scr, h_carry, *io_scratch,
                  final_norm, batch_major_in, batch_major_out):
    tc, bsz, d = t1.shape
    d_conv = dww_ref.shape[1]
    d_lru = lam_ref.shape[1]
    k_conv = dww_ref.shape[0]
    k_short = scw_ref.shape[0]
    halo_v = k_conv - 1
    halo_l = k_short - 1
    head_dim = wri_ref.shape[0] * 2
    n_heads = d_lru // head_dim
    cc = MXU_COLS
    n_chunks = d_conv // cc
    strips_per_chunk = cc // LANES
    n_tb = tc // TIME_BLOCK
    step = pl.program_id(0)
    n_steps = pl.num_programs(0)
    slot = lax.rem(step, 2)

    io_scratch = list(io_scratch)
    if batch_major_in:
        x_buf, in_sem = io_scratch[:2]
        io_scratch = io_scratch[2:]
    if batch_major_out:
        o_buf, out_sem = io_scratch[:2]

    @pl.when(step == 0)
    def _():
        v_scr[0:halo_v] = jnp.zeros((halo_v, bsz, d_conv), F32)
        lx_scr[0:halo_l] = jnp.zeros((halo_l, bsz, d_lru), F32)
        h_carry[...] = jnp.zeros((bsz, d_lru), F32)

    def in_copies(s, buf_slot):
        return [pltpu.make_async_copy(x_in.at[b, pl.ds(s * tc, tc), :],
                                      x_buf.at[buf_slot, :, b, :],
                                      in_sem.at[buf_slot, b]) for b in range(bsz)]

    def out_copies(s, buf_slot):
        return [pltpu.make_async_copy(o_buf.at[buf_slot, :, b, :],
                                      o_out.at[b, pl.ds(s * tc, tc), :],
                                      out_sem.at[buf_slot, b]) for b in range(bsz)]

    if batch_major_in:
        @pl.when(step == 0)
        def _():
            for cp in in_copies(0, 0):
                cp.start()

        @pl.when(step + 1 < n_steps)
        def _():
            for cp in in_copies(step + 1, 1 - slot):
                cp.start()

        for cp in in_copies(step, slot):
            cp.wait()

    def x_at(t):
        return x_buf[slot, t] if batch_major_in else x_in[t]

    def rows(t, n=1):
        return slice(t * bsz, (t + n) * bsz)

    def chunk(c):
        return slice(c * cc, (c + 1) * cc)

    def strip(s):
        return slice(s * LANES, (s + 1) * LANES)

    def matmul(lhs, packed_w):
        res = jnp.dot(lhs, pltpu.bitcast(packed_w, BF16),
                      preferred_element_type=F32)
        return res.reshape(tc, bsz, res.shape[-1])

    def norm(t):
        x = x_at(t)
        ms = jnp.mean(x * x, axis=-1, keepdims=True)
        xn = (x * lax.rsqrt(ms + EPS)) * ng_ref[...]
        h = xn * (1.0 + mod_ref[1]) + mod_ref[0]
        h_bf[rows(t), :] = h.astype(BF16)

    def pair(p):
        return slice(p * 2 * cc, (p + 1) * 2 * cc)

    def in_dot(col0):
        return matmul(h_bf[...], win_ref[:, col0:col0 + 2 * cc])

    def cvg_dot(c):
        res = in_dot(c * 2 * cc)
        cvg[c % 2, 0] = res[:, :, 0:cc]
        cvg[c % 2, 1] = res[:, :, cc:]

    def cs_dot(p):
        cs_scr[:, :, pair(p)] = in_dot(2 * d_conv + p * 2 * cc)

    def lx_dot(p):
        lx_scr[halo_l:halo_l + tc, :, pair(p)] = in_dot(3 * d_conv + p * 2 * cc)

    def ls_dot(p):
        ls_scr[:, :, pair(p)] = in_dot(3 * d_conv + d_lru + p * 2 * cc)

    def glu(c, tb):
        t0 = tb * TIME_BLOCK
        tt = slice(t0, t0 + TIME_BLOCK)
        hv = cvg[c % 2, 0, tt]
        v_scr[halo_v + t0:halo_v + t0 + TIME_BLOCK, :, chunk(c)] = (
            hv * jnp.tanh(cvg[c % 2, 1, tt]) + hv)

    def conv(tb, s):
        t0 = tb * TIME_BLOCK
        sl = strip(s)
        acc = dww_ref[0:1, sl] * v_scr[t0:t0 + TIME_BLOCK, :, sl]
        for j in range(1, k_conv):
            acc = acc + dww_ref[j:j + 1, sl] * v_scr[t0 + j:t0 + j + TIME_BLOCK, :, sl]
        conv_scr[t0:t0 + TIME_BLOCK, :, sl] = acc + dwb_ref[:, sl]

    def short_conv(tb, s):
        t0 = tb * TIME_BLOCK
        sl = strip(s)
        acc = scw_ref[0:1, sl] * lx_scr[t0:t0 + TIME_BLOCK, :, sl]
        for j in range(1, k_short):
            acc = acc + scw_ref[j:j + 1, sl] * lx_scr[t0 + j:t0 + j + TIME_BLOCK, :, sl]
        hx = acc + scb_ref[:, sl]
        xs_scr[t0:t0 + TIME_BLOCK, :, sl] = hx
        xs_bf[rows(t0, TIME_BLOCK), sl] = (
            hx.reshape(TIME_BLOCK * bsz, LANES).astype(BF16))

    def gate_dot(hd):
        c0 = hd * head_dim
        res = matmul(xs_bf[:, c0:c0 + head_dim],
                     wri_ref[:, 2 * c0:2 * (c0 + head_dim)])
        ri_scr[:, :, c0:c0 + head_dim] = res[:, :, 0:head_dim]
        ri_scr[:, :, d_lru + c0:d_lru + c0 + head_dim] = res[:, :, head_dim:]

    def layer_norm(t):
        v = conv_scr[t]
        mu = jnp.mean(v, axis=-1, keepdims=True)
        dv = v - mu
        var = jnp.mean(dv * dv, axis=-1, keepdims=True)
        hy = (dv * lax.rsqrt(var + EPS)) * clng_ref[...] + clnb_ref[...]
        ln_bf[rows(t), :] = _half_silu(hy).astype(BF16)

    def pw_dot(p):
        t1[:, :, pair(p)] = matmul(ln_bf[...], pw2w_ref[:, pair(p)])

    def conv_out(t):
        y = (t1[t] + pw2b_ref[...]) * _half_silu(cs_scr[t])
        ycat[rows(t), 0:d_conv] = y.astype(BF16)

    state = [None]

    def scan(t):
        ri = ri_scr[t]
        tr = jnp.tanh(ri[:, 0:d_lru] + br_ref[...])
        ti = jnp.tanh(ri[:, d_lru:] + bi_ref[...])
        hsp = sp_scr[...]
        a = jnp.exp(tr * hsp + hsp)
        mult = jnp.sqrt(1.0 - a * a)
        hx = xs_scr[t]
        h = a * state[0] + mult * (ti * hx + hx)
        state[0] = h
        ycat[rows(t), d_conv:] = (h * _half_silu(ls_scr[t])).astype(BF16)

    def out_dot_conv(p):
        t1[:, :, pair(p)] = matmul(ycat[:, 0:d_conv],
                                   wout_ref[0:d_conv // 2, pair(p)])

    def out_dot_lru(p):
        t1[:, :, pair(p)] += matmul(ycat[:, d_conv:],
                                    wout_ref[d_conv // 2:, pair(p)])

    def out(t):
        xo = x_at(t) + (1.0 + mod_ref[2]) * t1[t]
        if final_norm:
            ms = jnp.mean(xo * xo, axis=-1, keepdims=True)
            xo = (xo * lax.rsqrt(ms + EPS)) * fg_ref[...]
        if batch_major_out:
            o_buf[slot, t] = xo
        else:
            o_out[t] = xo

    dot_cost = 1024.0
    gate_cost = 256.0
    conv_cost = 274.0
    sc_cost = 35.0
    glu_cost = 40.0

    lam = lam_ref[...]
    sp_scr[...] = (-0.5 * LRU_C) * (jnp.maximum(-lam, 0.0)
                                    + jnp.log1p(jnp.exp(-jnp.abs(lam))))

    half_t = tc // 2

    def first_cvg_rows(t0):
        res = jnp.dot(h_bf[rows(t0, half_t), :],
                      pltpu.bitcast(win_ref[:, 0:2 * cc], BF16),
                      preferred_element_type=F32).reshape(half_t, bsz, 2 * cc)
        cvg[0, 0, t0:t0 + half_t] = res[:, :, 0:cc]
        cvg[0, 1, t0:t0 + half_t] = res[:, :, cc:]

    for t in range(half_t):
        norm(t)
    _emit_interleaved([(512.0, functools.partial(first_cvg_rows, 0))],
                      [(40.0, functools.partial(norm, t)) for t in range(half_t, tc)])
    first_cvg_rows(half_t)

    def dots(fn, ps):
        return [(dot_cost, functools.partial(fn, p)) for p in ps]

    def gate_dots(hds):
        return [(gate_cost, functools.partial(gate_dot, hd)) for hd in hds]

    def short_convs(strips):
        return [(sc_cost, functools.partial(short_conv, tb, s))
                for s in strips for tb in range(n_tb)]

    mxu_rounds = [
        dots(cvg_dot, [1]) + dots(lx_dot, [0]),
        dots(cvg_dot, [2]) + dots(lx_dot, [1]),
        dots(cvg_dot, [3]) + dots(cs_dot, [0]) + gate_dots(range(0, 4)),
        dots(cs_dot, [1]) + dots(ls_dot, [0, 1]) + gate_dots(range(4, 8)),
    ]
    valu_extra = [[], short_convs(range(0, 4)), short_convs(range(4, 8)), []]
    for r in range(n_chunks):
        valu = [(glu_cost, functools.partial(glu, r, tb)) for tb in range(n_tb)]
        for s in range(r * strips_per_chunk, (r + 1) * strips_per_chunk):
            valu += [(conv_cost, functools.partial(conv, tb, s)) for tb in range(n_tb)]
        _emit_interleaved(mxu_rounds[r], valu + valu_extra[r])

    for t in range(tc):
        layer_norm(t)

    state[0] = h_carry[...]
    half = tc // 2
    n_pairs = d // (2 * cc)
    _emit_interleaved(dots(pw_dot, range(n_pairs)),
                      [(90.0, functools.partial(scan, t)) for t in range(half)])
    for t in range(tc):
        conv_out(t)
    _emit_interleaved(dots(out_dot_conv, range(n_pairs)),
                      [(90.0, functools.partial(scan, t)) for t in range(half, tc)])
    h_carry[...] = state[0]
    for p in range(n_pairs):
        out_dot_lru(p)

    if batch_major_out:
        @pl.when(step >= 2)
        def _():
            for cp in out_copies(step - 2, slot):
                cp.wait()

    for t in range(tc):
        out(t)

    if batch_major_out:
        for cp in out_copies(step, slot):
            cp.start()

        @pl.when(step == n_steps - 1)
        def _():
            @pl.when(n_steps >= 2)
            def _():
                for cp in out_copies(step - 1, 1 - slot):
                    cp.wait()
            for cp in out_copies(step, slot):
                cp.wait()

    v_scr[0:halo_v] = v_scr[tc:tc + halo_v]
    lx_scr[0:halo_l] = lx_scr[tc:tc + halo_l]


def _resident(shape):
    zeros = (0,) * len(shape)
    return pl.BlockSpec(shape, lambda j: zeros, pipeline_mode=pl.Buffered(1))


def _layer(x, mod, ng, win, dww, dwb, clng, clnb, pw2w, pw2b, scw, scb, wri,
           br, bi, lam, wout, fg, *, final_norm, batch_major_in, batch_major_out):
    if batch_major_in:
        bsz, seq, d = x.shape
    else:
        seq, bsz, d = x.shape
    d_conv = dww.shape[1]
    d_lru = lam.shape[1]
    tc = TIME_CHUNK
    assert seq % tc == 0 and tc % (2 * TIME_BLOCK) == 0
    assert tc >= dww.shape[0] - 1 and tc >= scw.shape[0] - 1
    assert bsz % (2 * SUBLANES) == 0
    assert d % MXU_COLS == 0 and d_conv == d_lru and d_conv // MXU_COLS == 4
    assert wri.shape == (LANES // 2, 2 * d_lru) and d_lru // LANES == 8
    m = tc * bsz
    operands = (mod, ng, win, dww, dwb, clng, clnb, pw2w, pw2b, scw, scb, wri,
                br, bi, lam, wout, fg)
    time_major_block = pl.BlockSpec((tc, bsz, d), lambda j: (j, 0, 0))
    hbm = pl.BlockSpec(memory_space=pl.ANY)
    io_buffers = [pltpu.VMEM((2, tc, bsz, d), x.dtype),
                  pltpu.SemaphoreType.DMA((2, bsz))]
    out_dims = (bsz, seq, d) if batch_major_out else (seq, bsz, d)
    return pl.pallas_call(
        functools.partial(_layer_kernel, final_norm=final_norm,
                          batch_major_in=batch_major_in,
                          batch_major_out=batch_major_out),
        grid=(seq // tc,),
        in_specs=[hbm if batch_major_in else time_major_block]
        + [_resident(a.shape) for a in operands],
        out_specs=hbm if batch_major_out else time_major_block,
        out_shape=jax.ShapeDtypeStruct(out_dims, x.dtype),
        scratch_shapes=[
            pltpu.VMEM((m, d), BF16),
            pltpu.VMEM((m, d_conv), BF16),
            pltpu.VMEM((m, d_lru), BF16),
            pltpu.VMEM((2, 2, tc, bsz, MXU_COLS), F32),
            pltpu.VMEM((tc, bsz, d_conv), F32),
            pltpu.VMEM((tc, bsz, d_lru), F32),
            pltpu.VMEM((tc, bsz, d), F32),
            pltpu.VMEM((tc, bsz, d_conv), F32),
            pltpu.VMEM((tc, bsz, d_lru), F32),
            pltpu.VMEM((tc + dww.shape[0] - 1, bsz, d_conv), F32),
            pltpu.VMEM((tc + scw.shape[0] - 1, bsz, d_lru), F32),
            pltpu.VMEM((tc, bsz, 2 * d_lru), F32),
            pltpu.VMEM((m, d_conv + d_lru), BF16),
            pltpu.VMEM((1, d_lru), F32),
            pltpu.VMEM((bsz, d_lru), F32),
        ] + (io_buffers if batch_major_in else [])
        + (io_buffers if batch_major_out else []),
        compiler_params=pltpu.CompilerParams(
            dimension_semantics=("arbitrary",),
            vmem_limit_bytes=VMEM_LIMIT_BYTES),
        name="final_layer" if final_norm else "layer",
    )(x, *operands)


def _pack_weight(w, layer, src_chunk=lambda c: c, scale_of_chunk=None):
    _, k, n = w.shape
    cc = MXU_COLS
    return pl.pallas_call(
        functools.partial(_pack_kernel, scale_of_chunk=scale_of_chunk),
        grid=(n // cc,),
        in_specs=[pl.BlockSpec((None, k, cc), lambda c: (layer, 0, src_chunk(c)))],
        out_specs=pl.BlockSpec((k // 2, cc), lambda c: (0, c)),
        out_shape=jax.ShapeDtypeStruct((k // 2, n), jnp.uint32),
        name="pack_weight",
    )(w)


def _pack_kernel(w_ref, o_ref, *, scale_of_chunk):
    w = w_ref[...]
    if scale_of_chunk is not None:
        w = w * scale_of_chunk(pl.program_id(0))
    o_ref[...] = pltpu.bitcast(w.astype(BF16), jnp.uint32)


def kernel(x, c, norm_g, mod_w, mod_b, w_in, dw_w, dw_b, cln_g, cln_b, pw2_w,
           pw2_b, sc_w, sc_b, wr, br, wi, bi, lam, w_out, final_g):
    depth = norm_g.shape[0]
    d_conv = dw_w.shape[2]
    d_lru = lam.shape[1]
    mod = _modulation(c, mod_w, mod_b)
    row = lambda a: a.reshape(1, -1)
    fg = row(final_g)
    n_cv = d_conv // MXU_COLS
    n_lx = d_lru // MXU_COLS

    def w_in_src_chunk(c):
        return jnp.where(c < 2 * n_cv, c // 2 + (c % 2) * n_cv, c)

    def w_in_scale(c):
        return jnp.where((c >= 3 * n_cv) & (c < 3 * n_cv + n_lx), 1.0, 0.5)

    n_heads, head_dim = wr.shape[1], wr.shape[2]
    wri = jnp.swapaxes(jnp.concatenate([wr, wi], axis=-1), 1, 2)
    wri = wri.reshape(depth, head_dim, n_heads * 2 * head_dim)

    h = x
    for l in range(depth):
        h = _layer(
            h, mod[l], row(norm_g[l]),
            _pack_weight(w_in, l, w_in_src_chunk, w_in_scale), dw_w[l],
            row(dw_b[l]), row(0.5 * cln_g[l]), row(0.5 * cln_b[l]),
            _pack_weight(pw2_w, l),
            row(pw2_b[l]), 0.5 * sc_w[l], row(0.5 * sc_b[l]), _pack_weight(wri, l),
            row(0.5 * br[l]), row(0.5 * bi[l]), row(lam[l]),
            _pack_weight(w_out, l), fg, final_norm=(l == depth - 1),
            batch_major_in=(l == 0), batch_major_out=(l == depth - 1))
    return h
```
